```python
import math
import jax, jax.numpy as jnp
from jax import lax
import numpy as np

D_MODEL = 1024
BATCH = 4
SEQ = 4096
DEPTH = 2

GRID_W = 64
CTX_LEN = 256
N_BRANCH = 4
D_A = D_MODEL // 2
SC_K = 3
D_NA = D_MODEL // 2
NA_HEADS = 8
NA_HD = D_NA // NA_HEADS
NA_KH = 8
NA_KW = 16
D_HY = D_MODEL // 2
HY_EMB = 33
HY_FH = 64
HY_SHIFT = 0.05
HY_FAST = 0.3
HY_SLOW = 1.5
HY_TARGET = 1e-2
D_CF = D_MODEL // 2
CF_K = 31
D_FF = ((8 * D_MODEL + 3 * 256 - 1) // (3 * 256)) * 256
EPS = 1e-6
OFF_A = 0
OFF_Q = OFF_A + 3 * D_A
OFF_K = OFF_Q + D_NA
OFF_V = OFF_K + D_NA
OFF_HY = OFF_V + D_NA
OFF_CF = OFF_HY + 3 * D_HY
OFF_G = OFF_CF + 2 * D_CF
N_IN = OFF_G + N_BRANCH * D_MODEL

kernel_name = 'hybrid_conv_na_hyena_conformer_dit_block'


def rmsnorm(x, g):
    xf = x.astype(jnp.float32)
    y = xf * lax.rsqrt(jnp.mean(xf * xf, axis=-1, keepdims=True) + EPS)
    return y.astype(x.dtype) * g


def layernorm(x, g, b):
    xf = x.astype(jnp.float32)
    mu = jnp.mean(xf, axis=-1, keepdims=True)
    var = jnp.mean(jnp.square(xf - mu), axis=-1, keepdims=True)
    return ((xf - mu) * lax.rsqrt(var + EPS)).astype(x.dtype) * g + b


def modulate(h, shift, scale):
    return h * (1 + scale) + shift


def dwconv(x, w):
    k = w.shape[0]
    return lax.conv_general_dilated(x, w[:, None, :].astype(x.dtype), (1,), [((k - 1) // 2, k // 2)],
                                    dimension_numbers=('NWC', 'WIO', 'NWC'), feature_group_count=x.shape[-1])


def split_heads(t):
    return t.reshape(t.shape[:-1] + (NA_HEADS, NA_HD))


def na_context(qc, kc, vc):
    s = jnp.einsum('bqhd,bkhd->bhqk', qc, kc) * (NA_HD ** -0.5)
    p = jax.nn.softmax(s.astype(jnp.float32), axis=-1).astype(vc.dtype)
    return jnp.einsum('bhqk,bkhd->bqhd', p, vc)


def na_latent(q, k, v, kc, vc, rpb):
    b, s = q.shape[0], q.shape[1]
    rows = s // GRID_W
    kh = min(NA_KH, rows)
    qg = q.reshape(b, rows, GRID_W, NA_HEADS, NA_HD)
    kg = k.reshape(b, rows, GRID_W, NA_HEADS, NA_HD)
    vg = v.reshape(b, rows, GRID_W, NA_HEADS, NA_HD)
    cols = jnp.arange(GRID_W)
    cidx = jnp.clip(cols - NA_KW // 2, 0, GRID_W - NA_KW)[:, None] + jnp.arange(NA_KW)[None, :]
    dc = cidx - cols[:, None] + NA_KW - 1
    scale = NA_HD ** -0.5
    n_loc = kh * NA_KW

    def row_block(r):
        rs = jnp.clip(r - kh // 2, 0, rows - kh)
        dr = rs + jnp.arange(kh) - r + NA_KH - 1
        bias = rpb[:, dr][:, :, dc].transpose(0, 2, 1, 3)
        q_r = lax.dynamic_index_in_dim(qg, r, axis=1, keepdims=False)
        k_win = lax.dynamic_slice_in_dim(kg, rs, kh, axis=1)[:, :, cidx]
        v_win = lax.dynamic_slice_in_dim(vg, rs, kh, axis=1)[:, :, cidx]
        s_loc = jnp.einsum('bqhd,biqjhd->bhqij', q_r, k_win) * scale + bias[None]
        s_ctx = jnp.einsum('bqhd,bchd->bhqc', q_r, kc) * scale
        s_all = jnp.concatenate([s_loc.reshape(b, NA_HEADS, GRID_W, n_loc), s_ctx], axis=-1)
        p = jax.nn.softmax(s_all.astype(jnp.float32), axis=-1).astype(v.dtype)
        p_loc = p[..., :n_loc].reshape(b, NA_HEADS, GRID_W, kh, NA_KW)
        p_ctx = p[..., n_loc:]
        return (jnp.einsum('bhqij,biqjhd->bqhd', p_loc, v_win)
                + jnp.einsum('bhqc,bchd->bqhd', p_ctx, vc))

    out = lax.map(row_block, jnp.arange(rows))
    return out.transpose(1, 0, 2, 3, 4).reshape(b, s, D_NA)


def hyena_filter(length, w1, b1, w2, b2, w3, b3, w4, freq):
    f32 = jnp.float32
    t = jnp.linspace(0.0, 1.0, length, dtype=f32)[:, None]
    bands = (HY_EMB - 1) // 2
    w = 2.0 * math.pi * jnp.arange(length, dtype=f32)[:, None] / length
    f = jnp.linspace(1e-4, bands - 1, bands, dtype=f32)[None, :]
    z = jnp.concatenate([t, jnp.cos(f * w), -jnp.sin(f * w)], axis=-1)
    fr = freq.astype(f32)
    h = jnp.sin(fr * (z @ w1.astype(f32) + b1.astype(f32)))
    h = jnp.sin(fr * (h @ w2.astype(f32) + b2.astype(f32)))
    h = jnp.sin(fr * (h @ w3.astype(f32) + b3.astype(f32)))
    h = h @ w4.astype(f32)
    deltas = jnp.abs(jnp.linspace(math.log(HY_TARGET) / HY_SLOW, math.log(HY_TARGET) / HY_FAST, D_HY, dtype=f32))
    window = jnp.exp(-t * jnp.tile(deltas, 2)[None, :]) + HY_SHIFT
    return h * window


def bidir_fft_conv(u, filt):
    n_tok, ch = u.shape[1], u.shape[2]
    k_f, k_b = filt[:, :ch], filt[:, ch:]
    kern = jnp.concatenate([k_f, jnp.zeros((1, ch), filt.dtype), k_b[:0:-1]], axis=0)
    uf = jnp.fft.rfft(u.astype(jnp.float32), n=2 * n_tok, axis=1)
    kf = jnp.fft.rfft(kern, n=2 * n_tok, axis=0)
    y = jnp.fft.irfft(uf * kf[None], n=2 * n_tok, axis=1)[:, :n_tok]
    return y.astype(u.dtype)


def shortconv_branch(pa, conv_a, w_a_out):
    xin, bg, cg = jnp.split(pa, 3, axis=-1)
    return (bg * dwconv(cg * xin, conv_a)) @ w_a_out


def hyena_branch(ph, conv_h, filt, hy_skip, w_h_out):
    x0, x1, v = jnp.split(dwconv(ph, conv_h), 3, axis=-1)
    u = x1 * v
    return (x0 * (bidir_fft_conv(u, filt) + u * hy_skip)) @ w_h_out


def conformer_branch(pd, conv_d, conv_d_b, ln_g, ln_b, w_d_out):
    a, g = jnp.split(pd, 2, axis=-1)
    u = dwconv(a * jax.nn.sigmoid(g), conv_d) + conv_d_b
    return jax.nn.silu(layernorm(u, ln_g, ln_b)) @ w_d_out


def mixer(proj, attn, filt, conv_a, w_a_out, w_na_out, conv_h, hy_skip, w_h_out,
          conv_d, conv_d_b, ln_g, ln_b, w_d_out, w_o):
    y_a = shortconv_branch(proj[..., OFF_A:OFF_Q], conv_a, w_a_out)
    y_na = attn @ w_na_out
    y_hy = hyena_branch(proj[..., OFF_HY:OFF_CF], conv_h, filt, hy_skip, w_h_out)
    y_cf = conformer_branch(proj[..., OFF_CF:OFF_G], conv_d, conv_d_b, ln_g, ln_b, w_d_out)
    g = jax.nn.sigmoid(proj[..., OFF_G:])
    merged = (g[..., 0 * D_MODEL:1 * D_MODEL] * y_a + g[..., 1 * D_MODEL:2 * D_MODEL] * y_na
              + g[..., 2 * D_MODEL:3 * D_MODEL] * y_hy + g[..., 3 * D_MODEL:4 * D_MODEL] * y_cf)
    return merged @ w_o


def swiglu(h, w_ffn_in, w_ffn_out):
    a, up = jnp.split(h @ w_ffn_in, 2, axis=-1)
    return (jax.nn.silu(a) * up) @ w_ffn_out


def setup_inputs(seed: int = 0) -> dict:
    key = jax.random.key(seed)
    k = jax.random.split(key, 34)
    L = DEPTH

    def nrm(i, shape, scale):
        return jax.random.normal(k[i], shape, jnp.float32) * scale

    return {
        'x': nrm(0, (BATCH, SEQ, D_MODEL), 1.0),
        'c': nrm(1, (BATCH, D_MODEL), 1.0),
        'ctx': nrm(2, (BATCH, CTX_LEN, D_MODEL), 1.0),
        'c_ctx': nrm(3, (D_MODEL,), 1.0),
        'w_mod': nrm(4, (L, D_MODEL, 6 * D_MODEL), 0.5 * D_MODEL ** -0.5),
        'b_mod': nrm(5, (L, 6 * D_MODEL), 0.02),
        'g_norm1': 1.0 + nrm(6, (L, D_MODEL), 0.05),
        'g_norm2': 1.0 + nrm(7, (L, D_MODEL), 0.05),
        'w_in': nrm(8, (L, D_MODEL, N_IN), D_MODEL ** -0.5),
        'conv_a': nrm(9, (L, SC_K, D_A), SC_K ** -0.5),
        'w_a_out': nrm(10, (L, D_A, D_MODEL), D_A ** -0.5),
        'q_gain': 1.0 + nrm(11, (L, NA_HD), 0.05),
        'k_gain': 1.0 + nrm(12, (L, NA_HD), 0.05),
        'rpb': nrm(13, (L, NA_HEADS, 2 * NA_KH - 1, 2 * NA_KW - 1), 0.2),
        'w_na_out': nrm(14, (L, D_NA, D_MODEL), D_NA ** -0.5),
        'conv_h': nrm(15, (L, SC_K, 3 * D_HY), SC_K ** -0.5),
        'filt_w1': nrm(16, (L, HY_EMB, HY_FH), HY_EMB ** -0.5),
        'filt_b1': nrm(17, (L, HY_FH), 0.1),
        'filt_w2': nrm(18, (L, HY_FH, HY_FH), HY_FH ** -0.5),
        'filt_b2': nrm(19, (L, HY_FH), 0.1),
        'filt_w3': nrm(20, (L, HY_FH, HY_FH), HY_FH ** -0.5),
        'filt_b3': nrm(21, (L, HY_FH), 0.1),
        'filt_w4': nrm(22, (L, HY_FH, 2 * D_HY), 0.1 * HY_FH ** -0.5),
        'filt_freq': 1.0 + nrm(23, (L, HY_FH), 0.05),
        'hy_skip': nrm(24, (L, D_HY), 0.5),
        'w_h_out': nrm(25, (L, D_HY, D_MODEL), D_HY ** -0.5),
        'conv_d': nrm(26, (L, CF_K, D_CF), CF_K ** -0.5),
        'conv_d_b': nrm(27, (L, D_CF), 0.02),
        'ln_g': 1.0 + nrm(28, (L, D_CF), 0.05),
        'ln_b': nrm(29, (L, D_CF), 0.02),
        'w_d_out': nrm(30, (L, D_CF, D_MODEL), D_CF ** -0.5),
        'w_o': nrm(31, (L, D_MODEL, D_MODEL), D_MODEL ** -0.5),
        'w_ffn_in': nrm(32, (L, D_MODEL, 2 * D_FF), D_MODEL ** -0.5),
        'w_ffn_out': nrm(33, (L, D_FF, D_MODEL), D_FF ** -0.5),
    }


def reference(x, c, ctx, c_ctx, w_mod, b_mod, g_norm1, g_norm2, w_in, conv_a, w_a_out, q_gain, k_gain, rpb,
              w_na_out, conv_h, filt_w1, filt_b1, filt_w2, filt_b2, filt_w3, filt_b3, filt_w4, filt_freq,
              hy_skip, w_h_out, conv_d, conv_d_b, ln_g, ln_b, w_d_out, w_o, w_ffn_in, w_ffn_out):
    for l in range(DEPTH):
        last = l == DEPTH - 1

        def run_mixer(proj, attn, filt):
            return mixer(proj, attn, filt, conv_a[l], w_a_out[l], w_na_out[l], conv_h[l], hy_skip[l], w_h_out[l],
                         conv_d[l], conv_d_b[l], ln_g[l], ln_b[l], w_d_out[l], w_o[l])

        def make_filt(n_tok):
            return hyena_filter(n_tok, filt_w1[l], filt_b1[l], filt_w2[l], filt_b2[l], filt_w3[l], filt_b3[l],
                                filt_w4[l], filt_freq[l])

        mx = (jax.nn.silu(c) @ w_mod[l] + b_mod[l])[:, None, :]
        sh1, sc1, g1, sh2, sc2, g2 = jnp.split(mx, 6, axis=-1)
        mc = jax.nn.silu(c_ctx) @ w_mod[l] + b_mod[l]
        csh1, csc1, cg1, csh2, csc2, cg2 = jnp.split(mc, 6, axis=-1)

        hx = modulate(rmsnorm(x, g_norm1[l]), sh1, sc1)
        hc = modulate(rmsnorm(ctx, g_norm1[l]), csh1, csc1)
        px = hx @ w_in[l]
        q = rmsnorm(split_heads(px[..., OFF_Q:OFF_K]), q_gain[l])
        k = rmsnorm(split_heads(px[..., OFF_K:OFF_V]), k_gain[l])
        v = split_heads(px[..., OFF_V:OFF_HY])
        if last:
            kv_c = hc @ w_in[l][:, OFF_K:OFF_HY]
            kc_raw, vc_raw = kv_c[..., :D_NA], kv_c[..., D_NA:]
        else:
            pc = hc @ w_in[l]
            kc_raw, vc_raw = pc[..., OFF_K:OFF_V], pc[..., OFF_V:OFF_HY]
        kc = rmsnorm(split_heads(kc_raw), k_gain[l])
        vc = split_heads(vc_raw)

        attn_x = na_latent(q, k, v, kc, vc, rpb[l])
        x_mix = run_mixer(px, attn_x, make_filt(x.shape[1]))

        if not last:
            qc = rmsnorm(split_heads(pc[..., OFF_Q:OFF_K]), q_gain[l])
            attn_c = na_context(qc, kc, vc).reshape(ctx.shape[0], ctx.shape[1], D_NA)
            c_mix = run_mixer(pc, attn_c, make_filt(ctx.shape[1]))
            ctx = ctx + cg1 * c_mix
            ctx = ctx + cg2 * swiglu(modulate(rmsnorm(ctx, g_norm2[l]), csh2, csc2), w_ffn_in[l], w_ffn_out[l])

        x = x + g1 * x_mix
        x = x + g2 * swiglu(modulate(rmsnorm(x, g_norm2[l]), sh2, sc2), w_ffn_in[l], w_ffn_out[l])
    return x
```

```python
import functools
import math

import numpy as np
import jax
import jax.numpy as jnp
from jax import lax
from jax.experimental import pallas as pl
from jax.experimental.pallas import tpu as pltpu

F32 = jnp.float32
BF16 = jnp.bfloat16
HIGHEST = lax.Precision.HIGHEST

D_MODEL = 1024
GRID_W = 64
N_BRANCH = 4
D_A = D_MODEL // 2
SC_K = 3
D_NA = D_MODEL // 2
NA_HEADS = 8
NA_HD = D_NA // NA_HEADS
NA_KH = 8
NA_KW = 16
D_HY = D_MODEL // 2
HY_EMB = 33
HY_FH = 64
HY_SHIFT = 0.05
HY_FAST = 0.3
HY_SLOW = 1.5
HY_TARGET = 1e-2
D_CF = D_MODEL // 2
CF_K = 31
D_FF = ((8 * D_MODEL + 3 * 256 - 1) // (3 * 256)) * 256
EPS = 1e-6
OFF_A = 0
OFF_Q = OFF_A + 3 * D_A
OFF_K = OFF_Q + D_NA
OFF_V = OFF_K + D_NA
OFF_HY = OFF_V + D_NA
OFF_CF = OFF_HY + 3 * D_HY
OFF_G = OFF_CF + 2 * D_CF
N_IN = OFF_G + N_BRANCH * D_MODEL

CB = 512
HALO = 16
NEG = -1e30
VMEM_LIMIT = 56 * 1024 * 1024

QROWS = 8
WROWS = 16
N_ROWS = 64


def _cparams(sem):
    return pltpu.CompilerParams(dimension_semantics=sem, vmem_limit_bytes=VMEM_LIMIT)


def _sigmoid(x):
    return 1.0 / (1.0 + jnp.exp(-x))


def _silu(x):
    return x * _sigmoid(x)


def _rms(x):
    return x * lax.rsqrt(jnp.mean(x * x, axis=-1, keepdims=True) + EPS)


def _mod_rmsnorm(x, g, shift, scale):
    return (_rms(x) * g) * (1.0 + scale) + shift


def _mod_kernel(c_ref, w_ref, b_ref, o_ref):
    s = _silu(c_ref[...])
    o_ref[...] = jnp.dot(s, w_ref[...], preferred_element_type=F32, precision=HIGHEST) + b_ref[...]


def _modulation(cc, w, b):
    m, d = cc.shape
    n = w.shape[1]
    tn = 1536
    return pl.pallas_call(
        _mod_kernel,
        grid=(n // tn,),
        in_specs=[pl.BlockSpec((m, d), lambda j: (0, 0)),
                  pl.BlockSpec((d, tn), lambda j: (0, j)),
                  pl.BlockSpec((1, tn), lambda j: (0, j))],
        out_specs=pl.BlockSpec((m, tn), lambda j: (0, j)),
        out_shape=jax.ShapeDtypeStruct((m, n), F32),
        compiler_params=_cparams(("arbitrary",)),
        name="modulation",
    )(cc, w, b.reshape(1, n))


def _in_proj_kernel(x_ref, g_ref, sh_ref, sc_ref, w_ref, o_ref, h_ref):
    @pl.when(pl.program_id(2) == 0)
    def _():
        h_ref[...] = _mod_rmsnorm(x_ref[0], g_ref[...], sh_ref[0], sc_ref[0]).astype(BF16)

    o_ref[0] = jnp.dot(h_ref[...], w_ref[...], preferred_element_type=F32).astype(o_ref.dtype)


def _in_proj(x, g, sh, sc, w, tm, tn):
    b, s, d = x.shape
    n = w.shape[1]
    return pl.pallas_call(
        _in_proj_kernel,
        grid=(b, s // tm, n // tn),
        in_specs=[pl.BlockSpec((1, tm, d), lambda bi, i, j: (bi, i, 0)),
                  pl.BlockSpec((1, d), lambda bi, i, j: (0, 0)),
                  pl.BlockSpec((1, 1, d), lambda bi, i, j: (bi, 0, 0)),
                  pl.BlockSpec((1, 1, d), lambda bi, i, j: (bi, 0, 0)),
                  pl.BlockSpec((d, tn), lambda bi, i, j: (0, j))],
        out_specs=pl.BlockSpec((1, tm, tn), lambda bi, i, j: (bi, i, j)),
        out_shape=jax.ShapeDtypeStruct((b, s, n), BF16),
        scratch_shapes=[pltpu.VMEM((tm, d), BF16)],
        compiler_params=_cparams(("parallel", "parallel", "arbitrary")),
        name="in_proj",
    )(x, g.reshape(1, d), sh, sc, w)


def _attn_latent_kernel(q_ref, k_ref, v_ref, kc_ref, vc_ref, qg_ref, kg_ref, tab_ref, o_ref):
    j = pl.program_id(2)
    wr = jnp.clip(QROWS * j - (WROWS - QROWS) // 2, 0, N_ROWS - WROWS)
    start = pl.multiple_of(wr * GRID_W, 256)
    nwin = WROWS * GRID_W
    qn = (_rms(q_ref[0, 0].astype(F32)) * qg_ref[...] * (NA_HD ** -0.5)).astype(BF16)
    kn = (_rms(k_ref[0, 0, pl.ds(start, nwin), :].astype(F32)) * kg_ref[...]).astype(BF16)
    vw = v_ref[0, 0, pl.ds(start, nwin), :]
    kcn = (_rms(kc_ref[0, 0].astype(F32)) * kg_ref[...]).astype(BF16)
    dn = (((1,), (1,)), ((), ()))
    s_loc = lax.dot_general(qn, kn, dn, preferred_element_type=F32) + tab_ref[0, 0]
    s_ctx = lax.dot_general(qn, kcn, dn, preferred_element_type=F32)
    m = jnp.maximum(jnp.max(s_loc, axis=-1, keepdims=True), jnp.max(s_ctx, axis=-1, keepdims=True))
    p_loc = jnp.exp(s_loc - m)
    p_ctx = jnp.exp(s_ctx - m)
    den = jnp.sum(p_loc, axis=-1, keepdims=True) + jnp.sum(p_ctx, axis=-1, keepdims=True)
    o = (jnp.dot(p_loc.astype(BF16), vw, preferred_element_type=F32)
         + jnp.dot(p_ctx.astype(BF16), vc_ref[0, 0], preferred_element_type=F32))
    o_ref[0, 0] = (o / den).astype(o_ref.dtype)


def _attn_latent(q, k, v, kc, vc, qg, kg, table):
    b, h, s, hd = q.shape
    c = kc.shape[2]
    tq = QROWS * GRID_W
    nj = s // tq
    nwin = WROWS * GRID_W

    def tab_map(bi, hi, j):
        return (hi, jnp.where(j == 0, 0, jnp.where(j == nj - 1, 2, 1)), 0, 0)

    return pl.pallas_call(
        _attn_latent_kernel,
        grid=(b, h, nj),
        in_specs=[pl.BlockSpec((1, 1, tq, hd), lambda bi, hi, j: (bi, hi, j, 0)),
                  pl.BlockSpec((1, 1, s, hd), lambda bi, hi, j: (bi, hi, 0, 0)),
                  pl.BlockSpec((1, 1, s, hd), lambda bi, hi, j: (bi, hi, 0, 0)),
                  pl.BlockSpec((1, 1, c, hd), lambda bi, hi, j: (bi, hi, 0, 0)),
                  pl.BlockSpec((1, 1, c, hd), lambda bi, hi, j: (bi, hi, 0, 0)),
                  pl.BlockSpec((1, hd), lambda bi, hi, j: (0, 0)),
                  pl.BlockSpec((1, hd), lambda bi, hi, j: (0, 0)),
                  pl.BlockSpec((1, 1, tq, nwin), tab_map)],
        out_specs=pl.BlockSpec((1, 1, tq, hd), lambda bi, hi, j: (bi, hi, j, 0)),
        out_shape=jax.ShapeDtypeStruct((b, h, s, hd), BF16),
        compiler_params=_cparams(("parallel", "parallel", "arbitrary")),
        name="attn_latent",
    )(q, k, v, kc, vc, qg.reshape(1, hd), kg.reshape(1, hd), table)


def _attn_ctx_kernel(q_ref, k_ref, v_ref, qg_ref, kg_ref, o_ref):
    qn = (_rms(q_ref[0, 0].astype(F32)) * qg_ref[...] * (NA_HD ** -0.5)).astype(BF16)
    kn = (_rms(k_ref[0, 0].astype(F32)) * kg_ref[...]).astype(BF16)
    s = lax.dot_general(qn, kn, (((1,), (1,)), ((), ())), preferred_element_type=F32)
    m = jnp.max(s, axis=-1, keepdims=True)
    p = jnp.exp(s - m)
    den = jnp.sum(p, axis=-1, keepdims=True)
    o = jnp.dot(p.astype(BF16), v_ref[0, 0], preferred_element_type=F32)
    o_ref[0, 0] = (o / den).astype(o_ref.dtype)


def _attn_ctx(q, k, v, qg, kg):
    b, h, c, hd = q.shape
    spec = pl.BlockSpec((1, 1, c, hd), lambda bi, hi: (bi, hi, 0, 0))
    gspec = pl.BlockSpec((1, hd), lambda bi, hi: (0, 0))
    return pl.pallas_call(
        _attn_ctx_kernel,
        grid=(b, h),
        in_specs=[spec, spec, spec, gspec, gspec],
        out_specs=spec,
        out_shape=jax.ShapeDtypeStruct((b, h, c, hd), BF16),
        compiler_params=_cparams(("parallel", "parallel")),
        name="attn_ctx",
    )(q, k, v, qg.reshape(1, hd), kg.reshape(1, hd))


@functools.lru_cache(maxsize=None)
def _bias_index_tables():
    tq, nwin = QROWS * GRID_W, WROWS * GRID_W
    nj = N_ROWS // QROWS
    ql = np.arange(tq)[:, None]
    kl = np.arange(nwin)[None, :]
    dr_all, dc_all, ok_all = [], [], []
    for j in (0, 1, nj - 1):
        wr = int(np.clip(QROWS * j - (WROWS - QROWS) // 2, 0, N_ROWS - WROWS))
        r = QROWS * j + ql // GRID_W
        qc = ql % GRID_W
        kr = wr + kl // GRID_W
        kc = kl % GRID_W
        rs = np.clip(r - NA_KH // 2, 0, N_ROWS - NA_KH)
        cs = np.clip(qc - NA_KW // 2, 0, GRID_W - NA_KW)
        ok = (kr >= rs) & (kr < rs + NA_KH) & (kc >= cs) & (kc < cs + NA_KW)
        dr = np.where(ok, kr - r + NA_KH - 1, 0)
        dc = np.where(ok, kc - qc + NA_KW - 1, 0)
        dr_all.append(dr)
        dc_all.append(dc)
        ok_all.append(ok)
    return (np.stack(dr_all).astype(np.int32), np.stack(dc_all).astype(np.int32), np.stack(ok_all))


def _bias_table(rpb):
    dr, dc, ok = _bias_index_tables()
    return jnp.where(ok[None], rpb[:, dr, dc], NEG).astype(F32)


def _col_specs(off, width, tm, s):
    return [pl.BlockSpec((1, tm, CB), functools.partial(lambda bi, i, cb: (bi, i, cb), cb=off // CB + t))
            for t in range(width // CB)]


def _halo_specs(off, width, tm, s):
    r = tm // HALO
    last = s // HALO - 1
    specs = []
    for t in range(width // CB):
        cb = off // CB + t
        specs.append(pl.BlockSpec((1, HALO, CB),
                                  functools.partial(lambda bi, i, cb: (bi, jnp.maximum(i * r - 1, 0), cb), cb=cb)))
        specs.append(pl.BlockSpec((1, HALO, CB),
                                  functools.partial(lambda bi, i, cb: (bi, jnp.minimum((i + 1) * r, last), cb), cb=cb)))
    return specs


def _extend(main, prev, nxt, first, last):
    prev = jnp.where(first, 0.0, prev.astype(F32))
    nxt = jnp.where(last, 0.0, nxt.astype(F32))
    return jnp.concatenate([prev, main.astype(F32), nxt], axis=0)


def _dwconv_ext(ext, w, tm):
    k = w.shape[0]
    p = (k - 1) // 2
    n = ext.shape[0]
    acc = None
    for t in range(k):
        d = t - p
        sh = ext if d == 0 else pltpu.roll(ext, (-d) % n, 0)
        term = sh * w[t:t + 1, :]
        acc = term if acc is None else acc + term
    return acc[HALO:HALO + tm]


def _hy_pre_kernel(*refs, tm):
    mains = refs[0:3]
    halos = refs[3:9]
    w_ref = refs[9]
    x0_ref, u_ref = refs[10], refs[11]
    i = pl.program_id(1)
    first = i == 0
    last = i == pl.num_programs(1) - 1
    outs = []
    for t in range(3):
        ext = _extend(mains[t][0], halos[2 * t][0], halos[2 * t + 1][0], first, last)
        outs.append(_dwconv_ext(ext, w_ref[:, t * CB:(t + 1) * CB], tm))
    x0_ref[0] = outs[0].astype(x0_ref.dtype)
    u_ref[0] = outs[1] * outs[2]


def _hy_pre(px, conv_h, tm):
    b, s, _ = px.shape
    in_specs = (_col_specs(OFF_HY, 3 * D_HY, tm, s) + _halo_specs(OFF_HY, 3 * D_HY, tm, s)
                + [pl.BlockSpec((SC_K, 3 * D_HY), lambda bi, i: (0, 0))])
    ospec = pl.BlockSpec((1, tm, D_HY), lambda bi, i: (bi, i, 0))
    return pl.pallas_call(
        functools.partial(_hy_pre_kernel, tm=tm),
        grid=(b, s // tm),
        in_specs=in_specs,
        out_specs=[ospec, ospec],
        out_shape=[jax.ShapeDtypeStruct((b, s, D_HY), BF16), jax.ShapeDtypeStruct((b, s, D_HY), F32)],
        compiler_params=_cparams(("parallel", "parallel")),
        name="hyena_pre",
    )(*([px] * 9), conv_h)


@functools.lru_cache(maxsize=None)
def _filter_features(length):
    t = np.linspace(0.0, 1.0, length)[:, None]
    bands = (HY_EMB - 1) // 2
    w = 2.0 * math.pi * np.arange(length)[:, None] / length
    f = np.linspace(1e-4, bands - 1, bands)[None, :]
    z = np.concatenate([t, np.cos(f * w), -np.sin(f * w)], axis=-1)
    zp = np.zeros((length, HY_FH), np.float32)
    zp[:, :HY_EMB] = z
    deltas = np.abs(np.linspace(math.log(HY_TARGET) / HY_SLOW, math.log(HY_TARGET) / HY_FAST, D_HY))
    return zp, np.tile(deltas, 2)[None, :].astype(np.float32)


def _filter_kernel(z_ref, w1_ref, b1_ref, w2_ref, b2_ref, w3_ref, b3_ref, w4_ref, fr_ref, dl_ref, o_ref):
    z = z_ref[...]
    fr = fr_ref[...]
    dot = functools.partial(jnp.dot, preferred_element_type=F32, precision=HIGHEST)
    h = jnp.sin(fr * (dot(z, w1_ref[...]) + b1_ref[...]))
    h = jnp.sin(fr * (dot(h, w2_ref[...]) + b2_ref[...]))
    h = jnp.sin(fr * (dot(h, w3_ref[...]) + b3_ref[...]))
    h = dot(h, w4_ref[...])
    window = jnp.exp(-z[:, 0:1] * dl_ref[...]) + HY_SHIFT
    o_ref[...] = h * window


def _hyena_filter(length, w1, b1, w2, b2, w3, b3, w4, freq):
    zp, deltas = _filter_features(length)
    w1p = jnp.zeros((HY_FH, HY_FH), F32).at[:HY_EMB].set(w1)
    tl = min(length, 512)
    full = lambda shape: pl.BlockSpec(shape, lambda i: (0, 0))
    return pl.pallas_call(
        _filter_kernel,
        grid=(length // tl,),
        in_specs=[pl.BlockSpec((tl, HY_FH), lambda i: (i, 0)),
                  full((HY_FH, HY_FH)), full((1, HY_FH)), full((HY_FH, HY_FH)), full((1, HY_FH)),
                  full((HY_FH, HY_FH)), full((1, HY_FH)), full((HY_FH, 2 * D_HY)), full((1, HY_FH)),
                  full((1, 2 * D_HY))],
        out_specs=pl.BlockSpec((tl, 2 * D_HY), lambda i: (i, 0)),
        out_shape=jax.ShapeDtypeStruct((length, 2 * D_HY), F32),
        compiler_params=_cparams(("parallel",)),
        name="hyena_filter",
    )(jnp.asarray(zp), w1p, b1.reshape(1, -1), w2, b2.reshape(1, -1), w3, b3.reshape(1, -1), w4,
      freq.reshape(1, -1), jnp.asarray(deltas))


def _conv_kernel_sequence(filt):
    ch = filt.shape[1] // 2
    k_f, k_b = filt[:, :ch], filt[:, ch:]
    return jnp.concatenate([k_f, jnp.zeros((1, ch), filt.dtype), k_b[:0:-1]], axis=0)


FFT_N1 = 64
FFT_N2 = 128
FFT_N = FFT_N1 * FFT_N2


@functools.lru_cache(maxsize=None)
def _fft_constants():
    n1, n2, n = FFT_N1, FFT_N2, FFT_N
    k1 = np.arange(n1)[:, None]
    t1 = np.arange(n1)[None, :]
    a1 = 2.0 * math.pi * k1 * t1 / n1
    c1, s1 = np.cos(a1), np.sin(a1)
    h = n1 // 2
    w_data = np.zeros((n1, 2, 2 * h))
    w_data[:, 0, :h], w_data[:, 0, h:] = c1[:, :h], s1[:, :h]
    w_data[:, 1, :h], w_data[:, 1, h:] = -s1[:, :h], c1[:, :h]
    w_kern = np.stack([c1, -s1], axis=1)
    w_inv = np.zeros((2, h, n1, 2))
    w_inv[0, :, :, 0], w_inv[0, :, :, 1] = c1[:, :h].T, -s1[:, :h].T
    w_inv[1, :, :, 0], w_inv[1, :, :, 1] = s1[:, :h].T, c1[:, :h].T
    w_inv /= n
    t2 = np.arange(n2)[None, :]
    at = 2.0 * math.pi * k1 * t2 / n
    twr, twi = np.cos(at), -np.sin(at)
    k2 = np.arange(n2)[:, None]
    a2 = 2.0 * math.pi * k2 * t2 / n2
    c2, s2 = np.cos(a2), np.sin(a2)
    w2f = np.block([[c2, s2], [-s2, c2]])
    w2i = np.block([[c2, -s2], [s2, c2]])
    f = np.float32
    return dict(w_data=w_data.reshape(2 * n1, 2 * h).astype(f), w_kern=w_kern.reshape(2 * n1, n1).astype(f),
                w_inv=w_inv.reshape(2 * h, 2 * n1).astype(f), twr=twr[:, :, None].astype(f),
                twi=twi[:, :, None].astype(f), w2f=w2f.astype(f), w2i=w2i.astype(f))


def _lmm_kernel(w_ref, z_ref, o_ref):
    o_ref[0] = jnp.dot(w_ref[...], z_ref[0], preferred_element_type=F32, precision=HIGHEST)


def _left_matmul(w, z, tn):
    g, k, n = z.shape
    m = w.shape[0]
    return pl.pallas_call(
        _lmm_kernel,
        grid=(g, n // tn),
        in_specs=[pl.BlockSpec((m, k), lambda gi, j: (0, 0)),
                  pl.BlockSpec((1, k, tn), lambda gi, j: (gi, 0, j))],
        out_specs=pl.BlockSpec((1, m, tn), lambda gi, j: (gi, 0, j)),
        out_shape=jax.ShapeDtypeStruct((g, m, n), F32),
        compiler_params=_cparams(("parallel", "parallel")),
        name="dft_stage1",
    )(w, z)


def _twiddle_fwd(a_ref, twr_ref, twi_ref):
    n2 = FFT_N2
    ar, ai = a_ref[0, 0, :n2, :], a_ref[0, 0, n2:, :]
    twr, twi = twr_ref[0], twi_ref[0]
    return ar * twr - ai * twi, ar * twi + ai * twr


def _stage2(w_ref, re, im):
    n2 = FFT_N2
    dot = functools.partial(jnp.dot, preferred_element_type=F32, precision=HIGHEST)
    out = dot(w_ref[:, :n2], re) + dot(w_ref[:, n2:], im)
    return out[:n2], out[n2:]


def _kspec_kernel(a_ref, twr_ref, twi_ref, w2f_ref, o_ref):
    br, bi = _twiddle_fwd(a_ref, twr_ref, twi_ref)
    xr, xi = _stage2(w2f_ref, br, bi)
    o_ref[0, :FFT_N2, :] = xr
    o_ref[0, FFT_N2:, :] = xi


def _fft_mid_kernel(a_ref, kf_ref, twr_ref, twi_ref, w2f_ref, w2i_ref, o_ref):
    n2 = FFT_N2
    br, bi = _twiddle_fwd(a_ref, twr_ref, twi_ref)
    xr, xi = _stage2(w2f_ref, br, bi)
    kr, ki = kf_ref[0, :n2, :], kf_ref[0, n2:, :]
    yr = xr * kr - xi * ki
    yi = xr * ki + xi * kr
    cr, ci = _stage2(w2i_ref, yr, yi)
    twr, twi = twr_ref[0], twi_ref[0]
    o_ref[0, 0, :n2, :] = cr * twr + ci * twi
    o_ref[0, 0, n2:, :] = ci * twr - cr * twi


def _long_conv_latent(u, kern):
    b, l, ch = u.shape
    n1, n2 = FFT_N1, FFT_N2
    cst = _fft_constants()
    cols = n2 * ch
    tw_spec = pl.BlockSpec((1, n2, 1), lambda k1, p: (k1, 0, 0))
    w2_spec = pl.BlockSpec((2 * n2, 2 * n2), lambda k1, p: (0, 0))
    ak = _left_matmul(jnp.asarray(cst["w_kern"]), kern.reshape(1, n1, cols), 8192)
    kf = pl.pallas_call(
        _kspec_kernel,
        grid=(n1, 1),
        in_specs=[pl.BlockSpec((1, 1, 2 * n2, ch), lambda k1, p: (0, k1, 0, 0)), tw_spec, tw_spec, w2_spec],
        out_specs=pl.BlockSpec((1, 2 * n2, ch), lambda k1, p: (k1, 0, 0)),
        out_shape=jax.ShapeDtypeStruct((n1, 2 * n2, ch), F32),
        compiler_params=_cparams(("parallel", "arbitrary")),
        name="dft_kernel_spectrum",
    )(ak.reshape(1, n1, 2 * n2, ch), jnp.asarray(cst["twr"]), jnp.asarray(cst["twi"]), jnp.asarray(cst["w2f"]))
    npair = b // 2
    a = _left_matmul(jnp.asarray(cst["w_data"]), u.reshape(npair, n1, cols), 8192)
    d = pl.pallas_call(
        _fft_mid_kernel,
        grid=(n1, npair),
        in_specs=[pl.BlockSpec((1, 1, 2 * n2, ch), lambda k1, p: (p, k1, 0, 0)),
                  pl.BlockSpec((1, 2 * n2, ch), lambda k1, p: (k1, 0, 0)),
                  tw_spec, tw_spec, w2_spec, w2_spec],
        out_specs=pl.BlockSpec((1, 1, 2 * n2, ch), lambda k1, p: (p, k1, 0, 0)),
        out_shape=jax.ShapeDtypeStruct((npair, n1, 2 * n2, ch), F32),
        compiler_params=_cparams(("parallel", "arbitrary")),
        name="dft_mid",
    )(a.reshape(npair, n1, 2 * n2, ch), kf, jnp.asarray(cst["twr"]), jnp.asarray(cst["twi"]),
      jnp.asarray(cst["w2f"]), jnp.asarray(cst["w2i"]))
    y = _left_matmul(jnp.asarray(cst["w_inv"]), d.reshape(npair, 2 * n1, cols), 8192)
    return y.reshape(b, l, ch)


@functools.lru_cache(maxsize=None)
def _dft_constants(n):
    k = np.arange(n)[:, None]
    t = np.arange(n)[None, :]
    ang = 2.0 * math.pi * k * t / n
    return np.cos(ang).astype(np.float32), (-np.sin(ang)).astype(np.float32)


def _conv_ctx_kernel(u_ref, kern_ref, fr_ref, fi_ref, o_ref):
    l = u_ref.shape[1]
    n = 2 * l
    dot = functools.partial(jnp.dot, preferred_element_type=F32, precision=HIGHEST)
    fr, fi = fr_ref[...], fi_ref[...]
    kern = kern_ref[...]
    kr, ki = dot(fr, kern), dot(fi, kern)
    u = u_ref[0]
    ur, ui = dot(fr[:, :l], u), dot(fi[:, :l], u)
    yr = ur * kr - ui * ki
    yi = ur * ki + ui * kr
    o_ref[0] = (dot(fr[:l, :], yr) + dot(fi[:l, :], yi)) * (1.0 / n)


def _long_conv_ctx(u, kern):
    b, l, ch = u.shape
    n = 2 * l
    fr, fi = _dft_constants(n)
    return pl.pallas_call(
        _conv_ctx_kernel,
        grid=(b,),
        in_specs=[pl.BlockSpec((1, l, ch), lambda bi: (bi, 0, 0)),
                  pl.BlockSpec((n, ch), lambda bi: (0, 0)),
                  pl.BlockSpec((n, n), lambda bi: (0, 0)),
                  pl.BlockSpec((n, n), lambda bi: (0, 0))],
        out_specs=pl.BlockSpec((1, l, ch), lambda bi: (bi, 0, 0)),
        out_shape=jax.ShapeDtypeStruct((b, l, ch), F32),
        compiler_params=_cparams(("parallel",)),
        name="long_conv_ctx",
    )(u, kern, jnp.asarray(fr), jnp.asarray(fi))


def _mix_kernel(*refs, tm):
    it = iter(refs)
    x_ref, g1_ref = next(it), next(it)
    pa = [next(it) for _ in range(3)]
    pa_h = [next(it) for _ in range(6)]
    attn_ref, x0_ref, u_ref, yh_ref = next(it), next(it), next(it), next(it)
    pcf = [next(it) for _ in range(2)]
    pcf_h = [next(it) for _ in range(4)]
    pg = [next(it) for _ in range(8)]
    (conva_ref, wa_ref, wna_ref, skip_ref, wh_ref, convd_ref, convdb_ref, lng_ref, lnb_ref, wd_ref,
     wo_ref, o_ref) = [next(it) for _ in range(12)]

    i = pl.program_id(1)
    first = i == 0
    last = i == pl.num_programs(1) - 1
    dot = functools.partial(jnp.dot, preferred_element_type=F32)

    ext_x = _extend(pa[0][0], pa_h[0][0], pa_h[1][0], first, last)
    ext_c = _extend(pa[2][0], pa_h[4][0], pa_h[5][0], first, last)
    za = pa[1][0].astype(F32) * _dwconv_ext(ext_c * ext_x, conva_ref[...], tm)
    y_a = dot(za.astype(BF16), wa_ref[...])
    y_na = dot(attn_ref[0], wna_ref[...])
    u = u_ref[0]
    zh = x0_ref[0].astype(F32) * (yh_ref[0] + u * skip_ref[...])
    y_hy = dot(zh.astype(BF16), wh_ref[...])
    ext_a = _extend(pcf[0][0], pcf_h[0][0], pcf_h[1][0], first, last)
    ext_g = _extend(pcf[1][0], pcf_h[2][0], pcf_h[3][0], first, last)
    uc = _dwconv_ext(ext_a * _sigmoid(ext_g), convd_ref[...], tm) + convdb_ref[...]
    mu = jnp.mean(uc, axis=-1, keepdims=True)
    dv = uc - mu
    var = jnp.mean(dv * dv, axis=-1, keepdims=True)
    zc = _silu(dv * lax.rsqrt(var + EPS) * lng_ref[...] + lnb_ref[...])
    y_cf = dot(zc.astype(BF16), wd_ref[...])

    halves = []
    for hf in range(2):
        sl = slice(hf * CB, (hf + 1) * CB)
        halves.append(_sigmoid(pg[0 + hf][0].astype(F32)) * y_a[:, sl]
                      + _sigmoid(pg[2 + hf][0].astype(F32)) * y_na[:, sl]
                      + _sigmoid(pg[4 + hf][0].astype(F32)) * y_hy[:, sl]
                      + _sigmoid(pg[6 + hf][0].astype(F32)) * y_cf[:, sl])
    merged = jnp.concatenate(halves, axis=-1).astype(BF16)
    o_ref[0] = x_ref[0] + g1_ref[0] * dot(merged, wo_ref[...])


def _mix(x, g1, px, attn, x0, u, yh, wts, tm):
    b, s, d = x.shape
    row = lambda width: pl.BlockSpec((1, tm, width), lambda bi, i: (bi, i, 0))
    full2 = lambda a: pl.BlockSpec(a.shape, lambda bi, i: (0, 0))
    in_specs = ([row(d), pl.BlockSpec((1, 1, d), lambda bi, i: (bi, 0, 0))]
                + _col_specs(OFF_A, 3 * D_A, tm, s) + _halo_specs(OFF_A, 3 * D_A, tm, s)
                + [row(D_NA), row(D_HY), row(D_HY), row(D_HY)]
                + _col_specs(OFF_CF, 2 * D_CF, tm, s) + _halo_specs(OFF_CF, 2 * D_CF, tm, s)
                + _col_specs(OFF_G, N_BRANCH * D_MODEL, tm, s)
                + [full2(w) for w in wts])
    n_px = 3 + 6 + 2 + 4 + 8
    args = [x, g1] + [px] * 9 + [attn, x0, u, yh] + [px] * 14 + list(wts)
    assert len(args) == len(in_specs) and n_px == 23
    return pl.pallas_call(
        functools.partial(_mix_kernel, tm=tm),
        grid=(b, s // tm),
        in_specs=in_specs,
        out_specs=row(d),
        out_shape=jax.ShapeDtypeStruct((b, s, d), F32),
        compiler_params=_cparams(("parallel", "parallel")),
        name="mixer_out",
    )(*args)


def _ffn_kernel(x_ref, g_ref, sh_ref, sc_ref, gate_ref, wa_ref, wu_ref, wo_ref, o_ref, h_ref, acc_ref):
    f = pl.program_id(2)

    @pl.when(f == 0)
    def _():
        h_ref[...] = _mod_rmsnorm(x_ref[0], g_ref[...], sh_ref[0], sc_ref[0]).astype(BF16)
        acc_ref[...] = jnp.zeros_like(acc_ref)

    h = h_ref[...]
    a = jnp.dot(h, wa_ref[...], preferred_element_type=F32)
    up = jnp.dot(h, wu_ref[...], preferred_element_type=F32)
    act = (_silu(a) * up).astype(BF16)
    acc_ref[...] += jnp.dot(act, wo_ref[...], preferred_element_type=F32)

    @pl.when(f == pl.num_programs(2) - 1)
    def _():
        o_ref[0] = x_ref[0] + gate_ref[0] * acc_ref[...]


def _ffn(x, g, sh, sc, gate, w_in, w_out, tm, tf):
    b, s, d = x.shape
    dff = w_out.shape[0]
    nf = dff // tf
    vec = pl.BlockSpec((1, 1, d), lambda bi, i, f: (bi, 0, 0))
    return pl.pallas_call(
        _ffn_kernel,
        grid=(b, s // tm, nf),
        in_specs=[pl.BlockSpec((1, tm, d), lambda bi, i, f: (bi, i, 0)),
                  pl.BlockSpec((1, d), lambda bi, i, f: (0, 0)),
                  vec, vec, vec,
                  pl.BlockSpec((d, tf), lambda bi, i, f: (0, f)),
                  pl.BlockSpec((d, tf), lambda bi, i, f: (0, f + nf)),
                  pl.BlockSpec((tf, d), lambda bi, i, f: (f, 0))],
        out_specs=pl.BlockSpec((1, tm, d), lambda bi, i, f: (bi, i, 0)),
        out_shape=jax.ShapeDtypeStruct((b, s, d), F32),
        scratch_shapes=[pltpu.VMEM((tm, d), BF16), pltpu.VMEM((tm, d), F32)],
        compiler_params=_cparams(("parallel", "parallel", "arbitrary")),
        name="ffn",
    )(x, g.reshape(1, d), sh, sc, gate, w_in, w_in, w_out)


def _heads(t):
    b, s, _ = t.shape
    return t.reshape(b, s, NA_HEADS, NA_HD).transpose(0, 2, 1, 3)


def _unheads(t):
    b, h, s, hd = t.shape
    return t.transpose(0, 2, 1, 3).reshape(b, s, h * hd)


def kernel(x, c, ctx, c_ctx, w_mod, b_mod, g_norm1, g_norm2, w_in, conv_a, w_a_out, q_gain, k_gain, rpb, w_na_out, conv_h, filt_w1, filt_b1, filt_w2, filt_b2, filt_w3, filt_b3, filt_w4, filt_freq, hy_skip, w_h_out, conv_d, conv_d_b, ln_g, ln_b, w_d_out, w_o, w_ffn_in, w_ffn_out):
    depth = w_mod.shape[0]
    b, s, d = x.shape
    n_ctx = ctx.shape[1]
    cc = jnp.zeros((8, d), F32).at[:b].set(c).at[b].set(c_ctx)

    for l in range(depth):
        last = l == depth - 1
        mods = _modulation(cc, w_mod[l], b_mod[l])
        lat = [mods[:b, t * d:(t + 1) * d].reshape(b, 1, d) for t in range(6)]
        cxm = [jnp.broadcast_to(mods[b:b + 1, t * d:(t + 1) * d].reshape(1, 1, d), (b, 1, d)) for t in range(6)]
        sh1, sc1, g1, sh2, sc2, g2 = lat
        csh1, csc1, cg1, csh2, csc2, cg2 = cxm

        w_in_b = w_in[l].astype(BF16)
        mix_w = (conv_a[l], w_a_out[l].astype(BF16), w_na_out[l].astype(BF16), hy_skip[l].reshape(1, -1),
                 w_h_out[l].astype(BF16), conv_d[l], conv_d_b[l].reshape(1, -1), ln_g[l].reshape(1, -1),
                 ln_b[l].reshape(1, -1), w_d_out[l].astype(BF16), w_o[l].astype(BF16))
        w_ffn_in_b = w_ffn_in[l].astype(BF16)
        w_ffn_out_b = w_ffn_out[l].astype(BF16)
        filt_args = (filt_w1[l], filt_b1[l], filt_w2[l], filt_b2[l], filt_w3[l], filt_b3[l], filt_w4[l], filt_freq[l])

        px = _in_proj(x, g_norm1[l], sh1, sc1, w_in_b, 1024, 2432)
        if last:
            kv_c = _in_proj(ctx, g_norm1[l], csh1, csc1, w_in_b[:, OFF_K:OFF_HY], n_ctx, 2 * D_NA)
            kc_raw, vc_raw = kv_c[..., :D_NA], kv_c[..., D_NA:]
        else:
            pc = _in_proj(ctx, g_norm1[l], csh1, csc1, w_in_b, n_ctx, 2432)
            kc_raw, vc_raw = pc[..., OFF_K:OFF_V], pc[..., OFF_V:OFF_HY]
        kc_h, vc_h = _heads(kc_raw), _heads(vc_raw)

        table = _bias_table(rpb[l])
        attn_x = _unheads(_attn_latent(_heads(px[..., OFF_Q:OFF_K]), _heads(px[..., OFF_K:OFF_V]),
                                       _heads(px[..., OFF_V:OFF_HY]), kc_h, vc_h, q_gain[l], k_gain[l], table))

        x0, u = _hy_pre(px, conv_h[l], 512)
        kern = _conv_kernel_sequence(_hyena_filter(s, *filt_args))
        yh = _long_conv_latent(u, kern)
        x_new = _mix(x, g1, px, attn_x, x0, u, yh, mix_w, 256)

        if not last:
            attn_c = _unheads(_attn_ctx(_heads(pc[..., OFF_Q:OFF_K]), kc_h, vc_h, q_gain[l], k_gain[l]))
            x0c, uc = _hy_pre(pc, conv_h[l], n_ctx)
            kern_c = _conv_kernel_sequence(_hyena_filter(n_ctx, *filt_args))
            yhc = _long_conv_ctx(uc, kern_c)
            ctx = _mix(ctx, cg1, pc, attn_c, x0c, uc, yhc, mix_w, n_ctx)
            ctx = _ffn(ctx, g_norm2[l], csh2, csc2, cg2, w_ffn_in_b, w_ffn_out_b, n_ctx, 1408)

        x = _ffn(x_new, g_norm2[l], sh2, sc2, g2, w_ffn_in_b, w_ffn_out_b, 512, 1408)
    return x
```

```python
import functools
import math

import numpy as np
import jax
import jax.numpy as jnp
from jax import lax
from jax.experimental import pallas as pl
from jax.experimental.pallas import tpu as pltpu

F32 = jnp.float32
BF16 = jnp.bfloat16
HIGHEST = lax.Precision.HIGHEST

D_MODEL = 1024
GRID_W = 64
N_BRANCH = 4
D_A = D_MODEL // 2
SC_K = 3
D_NA = D_MODEL // 2
NA_HEADS = 8
NA_HD = D_NA // NA_HEADS
NA_KH = 8
NA_KW = 16
D_HY = D_MODEL // 2
HY_EMB = 33
HY_FH = 64
HY_SHIFT = 0.05
HY_FAST = 0.3
HY_SLOW = 1.5
HY_TARGET = 1e-2
D_CF = D_MODEL // 2
CF_K = 31
D_FF = ((8 * D_MODEL + 3 * 256 - 1) // (3 * 256)) * 256
EPS = 1e-6
OFF_A = 0
OFF_Q = OFF_A + 3 * D_A
OFF_K = OFF_Q + D_NA
OFF_V = OFF_K + D_NA
OFF_HY = OFF_V + D_NA
OFF_CF = OFF_HY + 3 * D_HY
OFF_G = OFF_CF + 2 * D_CF
N_IN = OFF_G + N_BRANCH * D_MODEL

CB = 512
HALO = 16
NEG = -1e30
VMEM_LIMIT = 56 * 1024 * 1024

QROWS = 8
WROWS = 16
N_ROWS = 64


def _cparams(sem):
    return pltpu.CompilerParams(dimension_semantics=sem, vmem_limit_bytes=VMEM_LIMIT)


def _sigmoid(x):
    return 1.0 / (1.0 + jnp.exp(-x))


def _silu(x):
    return x * _sigmoid(x)


def _rms(x):
    return x * lax.rsqrt(jnp.mean(x * x, axis=-1, keepdims=True) + EPS)


def _mod_rmsnorm(x, g, shift, scale):
    return (_rms(x) * g) * (1.0 + scale) + shift


def _mod_kernel(c_ref, w_ref, b_ref, o_ref):
    s = _silu(c_ref[...])
    o_ref[...] = jnp.dot(s, w_ref[...], preferred_element_type=F32, precision=HIGHEST) + b_ref[...]


def _modulation(cc, w, b):
    m, d = cc.shape
    n = w.shape[1]
    tn = 1536
    return pl.pallas_call(
        _mod_kernel,
        grid=(n // tn,),
        in_specs=[pl.BlockSpec((m, d), lambda j: (0, 0)),
                  pl.BlockSpec((d, tn), lambda j: (0, j)),
                  pl.BlockSpec((1, tn), lambda j: (0, j))],
        out_specs=pl.BlockSpec((m, tn), lambda j: (0, j)),
        out_shape=jax.ShapeDtypeStruct((m, n), F32),
        compiler_params=_cparams(("arbitrary",)),
        name="modulation",
    )(cc, w, b.reshape(1, n))


def _in_proj_kernel(x_ref, g_ref, sh_ref, sc_ref, w_ref, o_ref, h_ref):
    @pl.when(pl.program_id(2) == 0)
    def _():
        h_ref[...] = _mod_rmsnorm(x_ref[0], g_ref[...], sh_ref[0], sc_ref[0]).astype(BF16)

    o_ref[0] = jnp.dot(h_ref[...], w_ref[...], preferred_element_type=F32).astype(o_ref.dtype)


def _in_proj(x, g, sh, sc, w, tm, tn):
    b, s, d = x.shape
    n = w.shape[1]
    return pl.pallas_call(
        _in_proj_kernel,
        grid=(b, s // tm, n // tn),
        in_specs=[pl.BlockSpec((1, tm, d), lambda bi, i, j: (bi, i, 0)),
                  pl.BlockSpec((1, d), lambda bi, i, j: (0, 0)),
                  pl.BlockSpec((1, 1, d), lambda bi, i, j: (bi, 0, 0)),
                  pl.BlockSpec((1, 1, d), lambda bi, i, j: (bi, 0, 0)),
                  pl.BlockSpec((d, tn), lambda bi, i, j: (0, j))],
        out_specs=pl.BlockSpec((1, tm, tn), lambda bi, i, j: (bi, i, j)),
        out_shape=jax.ShapeDtypeStruct((b, s, n), BF16),
        scratch_shapes=[pltpu.VMEM((tm, d), BF16)],
        compiler_params=_cparams(("parallel", "parallel", "arbitrary")),
        name="in_proj",
    )(x, g.reshape(1, d), sh, sc, w)


def _attn_latent_kernel(q_ref, k_ref, v_ref, kc_ref, vc_ref, qg_ref, kg_ref, tab_ref, o_ref):
    j = pl.program_id(2)
    wr = jnp.clip(QROWS * j - (WROWS - QROWS) // 2, 0, N_ROWS - WROWS)
    start = pl.multiple_of(wr * GRID_W, 256)
    nwin = WROWS * GRID_W
    qn = (_rms(q_ref[0, 0].astype(F32)) * qg_ref[...] * (NA_HD ** -0.5)).astype(BF16)
    kn = (_rms(k_ref[0, 0, pl.ds(start, nwin), :].astype(F32)) * kg_ref[...]).astype(BF16)
    vw = v_ref[0, 0, pl.ds(start, nwin), :]
    kcn = (_rms(kc_ref[0, 0].astype(F32)) * kg_ref[...]).astype(BF16)
    dn = (((1,), (1,)), ((), ()))
    s_loc = lax.dot_general(qn, kn, dn, preferred_element_type=F32) + tab_ref[0, 0]
    s_ctx = lax.dot_general(qn, kcn, dn, preferred_element_type=F32)
    m = jnp.maximum(jnp.max(s_loc, axis=-1, keepdims=True), jnp.max(s_ctx, axis=-1, keepdims=True))
    p_loc = jnp.exp(s_loc - m)
    p_ctx = jnp.exp(s_ctx - m)
    den = jnp.sum(p_loc, axis=-1, keepdims=True) + jnp.sum(p_ctx, axis=-1, keepdims=True)
    o = (jnp.dot(p_loc.astype(BF16), vw, preferred_element_type=F32)
         + jnp.dot(p_ctx.astype(BF16), vc_ref[0, 0], preferred_element_type=F32))
    o_ref[0, 0] = (o / den).astype(o_ref.dtype)


def _attn_latent(q, k, v, kc, vc, qg, kg, table):
    b, h, s, hd = q.shape
    c = kc.shape[2]
    tq = QROWS * GRID_W
    nj = s // tq
    nwin = WROWS * GRID_W

    def tab_map(bi, hi, j):
        return (hi, jnp.where(j == 0, 0, jnp.where(j == nj - 1, 2, 1)), 0, 0)

    return pl.pallas_call(
        _attn_latent_kernel,
        grid=(b, h, nj),
        in_specs=[pl.BlockSpec((1, 1, tq, hd), lambda bi, hi, j: (bi, hi, j, 0)),
                  pl.BlockSpec((1, 1, s, hd), lambda bi, hi, j: (bi, hi, 0, 0)),
                  pl.BlockSpec((1, 1, s, hd), lambda bi, hi, j: (bi, hi, 0, 0)),
                  pl.BlockSpec((1, 1, c, hd), lambda bi, hi, j: (bi, hi, 0, 0)),
                  pl.BlockSpec((1, 1, c, hd), lambda bi, hi, j: (bi, hi, 0, 0)),
                  pl.BlockSpec((1, hd), lambda bi, hi, j: (0, 0)),
                  pl.BlockSpec((1, hd), lambda bi, hi, j: (0, 0)),
                  pl.BlockSpec((1, 1, tq, nwin), tab_map)],
        out_specs=pl.BlockSpec((1, 1, tq, hd), lambda bi, hi, j: (bi, hi, j, 0)),
        out_shape=jax.ShapeDtypeStruct((b, h, s, hd), BF16),
        compiler_params=_cparams(("parallel", "parallel", "arbitrary")),
        name="attn_latent",
    )(q, k, v, kc, vc, qg.reshape(1, hd), kg.reshape(1, hd), table)


def _attn_ctx_kernel(q_ref, k_ref, v_ref, qg_ref, kg_ref, o_ref):
    qn = (_rms(q_ref[0, 0].astype(F32)) * qg_ref[...] * (NA_HD ** -0.5)).astype(BF16)
    kn = (_rms(k_ref[0, 0].astype(F32)) * kg_ref[...]).astype(BF16)
    s = lax.dot_general(qn, kn, (((1,), (1,)), ((), ())), preferred_element_type=F32)
    m = jnp.max(s, axis=-1, keepdims=True)
    p = jnp.exp(s - m)
    den = jnp.sum(p, axis=-1, keepdims=True)
    o = jnp.dot(p.astype(BF16), v_ref[0, 0], preferred_element_type=F32)
    o_ref[0, 0] = (o / den).astype(o_ref.dtype)


def _attn_ctx(q, k, v, qg, kg):
    b, h, c, hd = q.shape
    spec = pl.BlockSpec((1, 1, c, hd), lambda bi, hi: (bi, hi, 0, 0))
    gspec = pl.BlockSpec((1, hd), lambda bi, hi: (0, 0))
    return pl.pallas_call(
        _attn_ctx_kernel,
        grid=(b, h),
        in_specs=[spec, spec, spec, gspec, gspec],
        out_specs=spec,
        out_shape=jax.ShapeDtypeStruct((b, h, c, hd), BF16),
        compiler_params=_cparams(("parallel", "parallel")),
        name="attn_ctx",
    )(q, k, v, qg.reshape(1, hd), kg.reshape(1, hd))


N_DR = 2 * NA_KH - 1
N_DC = 2 * NA_KW - 1
DR_SLOTS = 24
DC_PAD = 32


@functools.lru_cache(maxsize=None)
def _bias_constants():
    qc = np.arange(GRID_W)[:, None]
    kc = np.arange(2 * GRID_W)[None, :] % GRID_W
    cs = np.clip(qc - NA_KW // 2, 0, GRID_W - NA_KW)
    ok = (kc >= cs) & (kc < cs + NA_KW)
    dc = np.where(ok, kc - qc + NA_KW - 1, DC_PAD - 1)
    onehot = (np.arange(DC_PAD)[:, None, None] == dc[None]) & ok[None]
    colneg = np.where(ok, 0.0, NEG).reshape(1, -1)
    rowneg = np.full((DR_SLOTS, 1), NEG)
    rowneg[1:N_DR + 1] = 0.0
    f = np.float32
    return onehot.reshape(DC_PAD, -1).astype(f), colneg.astype(f), rowneg.astype(f)


def _bias_cols_kernel(rpb_ref, oh_ref, colneg_ref, rowneg_ref, o_ref):
    sel = jnp.dot(rpb_ref[0], oh_ref[...], preferred_element_type=F32, precision=HIGHEST)
    o_ref[0] = sel + colneg_ref[...] + rowneg_ref[...]


def _bias_table_kernel(m_ref, o_ref):
    left = lax.broadcasted_iota(jnp.int32, (GRID_W, 2 * GRID_W), 1) < GRID_W
    nj = N_ROWS // QROWS
    for ty, j in enumerate((0, 1, nj - 1)):
        wr = min(max(QROWS * j - (WROWS - QROWS) // 2, 0), N_ROWS - WROWS)
        for qr in range(QROWS):
            r = QROWS * j + qr
            rs = min(max(r - NA_KH // 2, 0), N_ROWS - NA_KH)
            slots = [kr - r + NA_KH if rs <= kr < rs + NA_KH else 0 for kr in range(wr, wr + WROWS)]
            for kp in range(WROWS // 2):
                e0, e1 = slots[2 * kp], slots[2 * kp + 1]
                blk = m_ref[0, e0] if e0 == e1 else jnp.where(left, m_ref[0, e0], m_ref[0, e1])
                o_ref[0, ty, qr * GRID_W:(qr + 1) * GRID_W, kp * 2 * GRID_W:(kp + 1) * 2 * GRID_W] = blk


def _bias_table(rpb):
    h = rpb.shape[0]
    onehot, colneg, rowneg = _bias_constants()
    rp = jnp.zeros((h, DR_SLOTS, DC_PAD), F32).at[:, 1:N_DR + 1, :N_DC].set(rpb)
    ncol = GRID_W * 2 * GRID_W
    cols = pl.pallas_call(
        _bias_cols_kernel,
        grid=(h,),
        in_specs=[pl.BlockSpec((1, DR_SLOTS, DC_PAD), lambda hi: (hi, 0, 0)),
                  pl.BlockSpec((DC_PAD, ncol), lambda hi: (0, 0)),
                  pl.BlockSpec((1, ncol), lambda hi: (0, 0)),
                  pl.BlockSpec((DR_SLOTS, 1), lambda hi: (0, 0))],
        out_specs=pl.BlockSpec((1, DR_SLOTS, ncol), lambda hi: (hi, 0, 0)),
        out_shape=jax.ShapeDtypeStruct((h, DR_SLOTS, ncol), F32),
        compiler_params=_cparams(("parallel",)),
        name="bias_cols",
    )(rp, jnp.asarray(onehot), jnp.asarray(colneg), jnp.asarray(rowneg))
    tq, nwin = QROWS * GRID_W, WROWS * GRID_W
    return pl.pallas_call(
        _bias_table_kernel,
        grid=(h,),
        in_specs=[pl.BlockSpec((1, DR_SLOTS, GRID_W, 2 * GRID_W), lambda hi: (hi, 0, 0, 0))],
        out_specs=pl.BlockSpec((1, 3, tq, nwin), lambda hi: (hi, 0, 0, 0)),
        out_shape=jax.ShapeDtypeStruct((h, 3, tq, nwin), F32),
        compiler_params=_cparams(("parallel",)),
        name="bias_table",
    )(cols.reshape(h, DR_SLOTS, GRID_W, 2 * GRID_W))


def _col_specs(off, width, tm, s):
    return [pl.BlockSpec((1, tm, CB), functools.partial(lambda bi, i, cb: (bi, i, cb), cb=off // CB + t))
            for t in range(width // CB)]


def _halo_specs(off, width, tm, s):
    r = tm // HALO
    last = s // HALO - 1
    specs = []
    for t in range(width // CB):
        cb = off // CB + t
        specs.append(pl.BlockSpec((1, HALO, CB),
                                  functools.partial(lambda bi, i, cb: (bi, jnp.maximum(i * r - 1, 0), cb), cb=cb)))
        specs.append(pl.BlockSpec((1, HALO, CB),
                                  functools.partial(lambda bi, i, cb: (bi, jnp.minimum((i + 1) * r, last), cb), cb=cb)))
    return specs


def _extend(main, prev, nxt, first, last):
    prev = jnp.where(first, 0.0, prev.astype(F32))
    nxt = jnp.where(last, 0.0, nxt.astype(F32))
    return jnp.concatenate([prev, main.astype(F32), nxt], axis=0)


def _dwconv_ext(ext, w, tm):
    k = w.shape[0]
    p = (k - 1) // 2
    n = ext.shape[0]
    acc = None
    for t in range(k):
        d = t - p
        sh = ext if d == 0 else pltpu.roll(ext, (-d) % n, 0)
        term = sh * w[t:t + 1, :]
        acc = term if acc is None else acc + term
    return acc[HALO:HALO + tm]


def _hy_pre_kernel(*refs, tm):
    mains = refs[0:3]
    halos = refs[3:9]
    w_ref = refs[9]
    x0_ref, u_ref = refs[10], refs[11]
    i = pl.program_id(1)
    first = i == 0
    last = i == pl.num_programs(1) - 1
    outs = []
    for t in range(3):
        ext = _extend(mains[t][0], halos[2 * t][0], halos[2 * t + 1][0], first, last)
        outs.append(_dwconv_ext(ext, w_ref[:, t * CB:(t + 1) * CB], tm))
    x0_ref[0] = outs[0].astype(x0_ref.dtype)
    u_ref[0] = outs[1] * outs[2]


def _hy_pre(px, conv_h, tm):
    b, s, _ = px.shape
    in_specs = (_col_specs(OFF_HY, 3 * D_HY, tm, s) + _halo_specs(OFF_HY, 3 * D_HY, tm, s)
                + [pl.BlockSpec((SC_K, 3 * D_HY), lambda bi, i: (0, 0))])
    ospec = pl.BlockSpec((1, tm, D_HY), lambda bi, i: (bi, i, 0))
    return pl.pallas_call(
        functools.partial(_hy_pre_kernel, tm=tm),
        grid=(b, s // tm),
        in_specs=in_specs,
        out_specs=[ospec, ospec],
        out_shape=[jax.ShapeDtypeStruct((b, s, D_HY), BF16), jax.ShapeDtypeStruct((b, s, D_HY), F32)],
        compiler_params=_cparams(("parallel", "parallel")),
        name="hyena_pre",
    )(*([px] * 9), conv_h)


@functools.lru_cache(maxsize=None)
def _filter_features(length):
    t = np.linspace(0.0, 1.0, length)[:, None]
    bands = (HY_EMB - 1) // 2
    w = 2.0 * math.pi * np.arange(length)[:, None] / length
    f = np.linspace(1e-4, bands - 1, bands)[None, :]
    z = np.concatenate([t, np.cos(f * w), -np.sin(f * w)], axis=-1)
    zp = np.zeros((length, HY_FH), np.float32)
    zp[:, :HY_EMB] = z
    deltas = np.abs(np.linspace(math.log(HY_TARGET) / HY_SLOW, math.log(HY_TARGET) / HY_FAST, D_HY))
    return zp, np.tile(deltas, 2)[None, :].astype(np.float32)


def _filter_kernel(z_ref, w1_ref, b1_ref, w2_ref, b2_ref, w3_ref, b3_ref, w4_ref, fr_ref, dl_ref, o_ref):
    z = z_ref[...]
    fr = fr_ref[...]
    dot = functools.partial(jnp.dot, preferred_element_type=F32, precision=HIGHEST)
    h = jnp.sin(fr * (dot(z, w1_ref[...]) + b1_ref[...]))
    h = jnp.sin(fr * (dot(h, w2_ref[...]) + b2_ref[...]))
    h = jnp.sin(fr * (dot(h, w3_ref[...]) + b3_ref[...]))
    h = dot(h, w4_ref[...])
    window = jnp.exp(-z[:, 0:1] * dl_ref[...]) + HY_SHIFT
    o_ref[...] = h * window


def _hyena_filter(length, w1, b1, w2, b2, w3, b3, w4, freq):
    zp, deltas = _filter_features(length)
    w1p = jnp.zeros((HY_FH, HY_FH), F32).at[:HY_EMB].set(w1)
    tl = min(length, 512)
    full = lambda shape: pl.BlockSpec(shape, lambda i: (0, 0))
    return pl.pallas_call(
        _filter_kernel,
        grid=(length // tl,),
        in_specs=[pl.BlockSpec((tl, HY_FH), lambda i: (i, 0)),
                  full((HY_FH, HY_FH)), full((1, HY_FH)), full((HY_FH, HY_FH)), full((1, HY_FH)),
                  full((HY_FH, HY_FH)), full((1, HY_FH)), full((HY_FH, 2 * D_HY)), full((1, HY_FH)),
                  full((1, 2 * D_HY))],
        out_specs=pl.BlockSpec((tl, 2 * D_HY), lambda i: (i, 0)),
        out_shape=jax.ShapeDtypeStruct((length, 2 * D_HY), F32),
        compiler_params=_cparams(("parallel",)),
        name="hyena_filter",
    )(jnp.asarray(zp), w1p, b1.reshape(1, -1), w2, b2.reshape(1, -1), w3, b3.reshape(1, -1), w4,
      freq.reshape(1, -1), jnp.asarray(deltas))


def _conv_kernel_sequence(filt):
    ch = filt.shape[1] // 2
    k_f, k_b = filt[:, :ch], filt[:, ch:]
    return jnp.concatenate([k_f, jnp.zeros((1, ch), filt.dtype), k_b[:0:-1]], axis=0)


FFT_N1 = 64
FFT_N2 = 128
FFT_N = FFT_N1 * FFT_N2


@functools.lru_cache(maxsize=None)
def _fft_constants():
    n1, n2, n = FFT_N1, FFT_N2, FFT_N
    k1 = np.arange(n1)[:, None]
    t1 = np.arange(n1)[None, :]
    a1 = 2.0 * math.pi * k1 * t1 / n1
    c1, s1 = np.cos(a1), np.sin(a1)
    h = n1 // 2
    w_data = np.zeros((n1, 2, 2 * h))
    w_data[:, 0, :h], w_data[:, 0, h:] = c1[:, :h], s1[:, :h]
    w_data[:, 1, :h], w_data[:, 1, h:] = -s1[:, :h], c1[:, :h]
    w_kern = np.stack([c1, -s1], axis=1)
    w_inv = np.zeros((2, h, n1, 2))
    w_inv[0, :, :, 0], w_inv[0, :, :, 1] = c1[:, :h].T, -s1[:, :h].T
    w_inv[1, :, :, 0], w_inv[1, :, :, 1] = s1[:, :h].T, c1[:, :h].T
    w_inv /= n
    t2 = np.arange(n2)[None, :]
    at = 2.0 * math.pi * k1 * t2 / n
    twr, twi = np.cos(at), -np.sin(at)
    k2 = np.arange(n2)[:, None]
    a2 = 2.0 * math.pi * k2 * t2 / n2
    c2, s2 = np.cos(a2), np.sin(a2)
    w2f = np.block([[c2, s2], [-s2, c2]])
    w2i = np.block([[c2, -s2], [s2, c2]])
    f = np.float32
    return dict(w_data=w_data.reshape(2 * n1, 2 * h).astype(f), w_kern=w_kern.reshape(2 * n1, n1).astype(f),
                w_inv=w_inv.reshape(2 * h, 2 * n1).astype(f), twr=twr[:, :, None].astype(f),
                twi=twi[:, :, None].astype(f), w2f=w2f.astype(f), w2i=w2i.astype(f))


def _lmm_kernel(w_ref, z_ref, o_ref):
    o_ref[0] = jnp.dot(w_ref[...], z_ref[0], preferred_element_type=F32, precision=HIGHEST)


def _left_matmul(w, z, tn):
    g, k, n = z.shape
    m = w.shape[0]
    return pl.pallas_call(
        _lmm_kernel,
        grid=(g, n // tn),
        in_specs=[pl.BlockSpec((m, k), lambda gi, j: (0, 0)),
                  pl.BlockSpec((1, k, tn), lambda gi, j: (gi, 0, j))],
        out_specs=pl.BlockSpec((1, m, tn), lambda gi, j: (gi, 0, j)),
        out_shape=jax.ShapeDtypeStruct((g, m, n), F32),
        compiler_params=_cparams(("parallel", "parallel")),
        name="dft_stage1",
    )(w, z)


def _twiddle_fwd(a_ref, twr_ref, twi_ref):
    n2 = FFT_N2
    ar, ai = a_ref[0, 0, :n2, :], a_ref[0, 0, n2:, :]
    twr, twi = twr_ref[0], twi_ref[0]
    return ar * twr - ai * twi, ar * twi + ai * twr


def _stage2(w_ref, re, im):
    n2 = FFT_N2
    dot = functools.partial(jnp.dot, preferred_element_type=F32, precision=HIGHEST)
    out = dot(w_ref[:, :n2], re) + dot(w_ref[:, n2:], im)
    return out[:n2], out[n2:]


def _kspec_kernel(a_ref, twr_ref, twi_ref, w2f_ref, o_ref):
    br, bi = _twiddle_fwd(a_ref, twr_ref, twi_ref)
    xr, xi = _stage2(w2f_ref, br, bi)
    o_ref[0, :FFT_N2, :] = xr
    o_ref[0, FFT_N2:, :] = xi


def _fft_mid_kernel(a_ref, kf_ref, twr_ref, twi_ref, w2f_ref, w2i_ref, o_ref):
    n2 = FFT_N2
    br, bi = _twiddle_fwd(a_ref, twr_ref, twi_ref)
    xr, xi = _stage2(w2f_ref, br, bi)
    kr, ki = kf_ref[0, :n2, :], kf_ref[0, n2:, :]
    yr = xr * kr - xi * ki
    yi = xr * ki + xi * kr
    cr, ci = _stage2(w2i_ref, yr, yi)
    twr, twi = twr_ref[0], twi_ref[0]
    o_ref[0, 0, :n2, :] = cr * twr + ci * twi
    o_ref[0, 0, n2:, :] = ci * twr - cr * twi


def _long_conv_latent(u, kern):
    b, l, ch = u.shape
    n1, n2 = FFT_N1, FFT_N2
    cst = _fft_constants()
    cols = n2 * ch
    tw_spec = pl.BlockSpec((1, n2, 1), lambda k1, p: (k1, 0, 0))
    w2_spec = pl.BlockSpec((2 * n2, 2 * n2), lambda k1, p: (0, 0))
    ak = _left_matmul(jnp.asarray(cst["w_kern"]), kern.reshape(1, n1, cols), 8192)
    kf = pl.pallas_call(
        _kspec_kernel,
        grid=(n1, 1),
        in_specs=[pl.BlockSpec((1, 1, 2 * n2, ch), lambda k1, p: (0, k1, 0, 0)), tw_spec, tw_spec, w2_spec],
        out_specs=pl.BlockSpec((1, 2 * n2, ch), lambda k1, p: (k1, 0, 0)),
        out_shape=jax.ShapeDtypeStruct((n1, 2 * n2, ch), F32),
        compiler_params=_cparams(("parallel", "arbitrary")),
        name="dft_kernel_spectrum",
    )(ak.reshape(1, n1, 2 * n2, ch), jnp.asarray(cst["twr"]), jnp.asarray(cst["twi"]), jnp.asarray(cst["w2f"]))
    npair = b // 2
    a = _left_matmul(jnp.asarray(cst["w_data"]), u.reshape(npair, n1, cols), 8192)
    d = pl.pallas_call(
        _fft_mid_kernel,
        grid=(n1, npair),
        in_specs=[pl.BlockSpec((1, 1, 2 * n2, ch), lambda k1, p: (p, k1, 0, 0)),
                  pl.BlockSpec((1, 2 * n2, ch), lambda k1, p: (k1, 0, 0)),
                  tw_spec, tw_spec, w2_spec, w2_spec],
        out_specs=pl.BlockSpec((1, 1, 2 * n2, ch), lambda k1, p: (p, k1, 0, 0)),
        out_shape=jax.ShapeDtypeStruct((npair, n1, 2 * n2, ch), F32),
        compiler_params=_cparams(("parallel", "arbitrary")),
        name="dft_mid",
    )(a.reshape(npair, n1, 2 * n2, ch), kf, jnp.asarray(cst["twr"]), jnp.asarray(cst["twi"]),
      jnp.asarray(cst["w2f"]), jnp.asarray(cst["w2i"]))
    y = _left_matmul(jnp.asarray(cst["w_inv"]), d.reshape(npair, 2 * n1, cols), 8192)
    return y.reshape(b, l, ch)


@functools.lru_cache(maxsize=None)
def _dft_constants(n):
    k = np.arange(n)[:, None]
    t = np.arange(n)[None, :]
    ang = 2.0 * math.pi * k * t / n
    return np.cos(ang).astype(np.float32), (-np.sin(ang)).astype(np.float32)


def _conv_ctx_kernel(u_ref, kern_ref, fr_ref, fi_ref, o_ref):
    l = u_ref.shape[1]
    n = 2 * l
    dot = functools.partial(jnp.dot, preferred_element_type=F32, precision=HIGHEST)
    fr, fi = fr_ref[...], fi_ref[...]
    kern = kern_ref[...]
    kr, ki = dot(fr, kern), dot(fi, kern)
    u = u_ref[0]
    ur, ui = dot(fr[:, :l], u), dot(fi[:, :l], u)
    yr = ur * kr - ui * ki
    yi = ur * ki + ui * kr
    o_ref[0] = (dot(fr[:l, :], yr) + dot(fi[:l, :], yi)) * (1.0 / n)


def _long_conv_ctx(u, kern):
    b, l, ch = u.shape
    n = 2 * l
    fr, fi = _dft_constants(n)
    return pl.pallas_call(
        _conv_ctx_kernel,
        grid=(b,),
        in_specs=[pl.BlockSpec((1, l, ch), lambda bi: (bi, 0, 0)),
                  pl.BlockSpec((n, ch), lambda bi: (0, 0)),
                  pl.BlockSpec((n, n), lambda bi: (0, 0)),
                  pl.BlockSpec((n, n), lambda bi: (0, 0))],
        out_specs=pl.BlockSpec((1, l, ch), lambda bi: (bi, 0, 0)),
        out_shape=jax.ShapeDtypeStruct((b, l, ch), F32),
        compiler_params=_cparams(("parallel",)),
        name="long_conv_ctx",
    )(u, kern, jnp.asarray(fr), jnp.asarray(fi))


def _mix_kernel(*refs, tm):
    it = iter(refs)
    x_ref, g1_ref = next(it), next(it)
    pa = [next(it) for _ in range(3)]
    pa_h = [next(it) for _ in range(6)]
    attn_ref, x0_ref, u_ref, yh_ref = next(it), next(it), next(it), next(it)
    pcf = [next(it) for _ in range(2)]
    pcf_h = [next(it) for _ in range(4)]
    pg = [next(it) for _ in range(8)]
    (conva_ref, wa_ref, wna_ref, skip_ref, wh_ref, convd_ref, convdb_ref, lng_ref, lnb_ref, wd_ref,
     wo_ref, o_ref) = [next(it) for _ in range(12)]

    i = pl.program_id(1)
    first = i == 0
    last = i == pl.num_programs(1) - 1
    dot = functools.partial(jnp.dot, preferred_element_type=F32)

    ext_x = _extend(pa[0][0], pa_h[0][0], pa_h[1][0], first, last)
    ext_c = _extend(pa[2][0], pa_h[4][0], pa_h[5][0], first, last)
    za = pa[1][0].astype(F32) * _dwconv_ext(ext_c * ext_x, conva_ref[...], tm)
    y_a = dot(za.astype(BF16), wa_ref[...])
    y_na = dot(attn_ref[0], wna_ref[...])
    u = u_ref[0]
    zh = x0_ref[0].astype(F32) * (yh_ref[0] + u * skip_ref[...])
    y_hy = dot(zh.astype(BF16), wh_ref[...])
    ext_a = _extend(pcf[0][0], pcf_h[0][0], pcf_h[1][0], first, last)
    ext_g = _extend(pcf[1][0], pcf_h[2][0], pcf_h[3][0], first, last)
    uc = _dwconv_ext(ext_a * _sigmoid(ext_g), convd_ref[...], tm) + convdb_ref[...]
    mu = jnp.mean(uc, axis=-1, keepdims=True)
    dv = uc - mu
    var = jnp.mean(dv * dv, axis=-1, keepdims=True)
    zc = _silu(dv * lax.rsqrt(var + EPS) * lng_ref[...] + lnb_ref[...])
    y_cf = dot(zc.astype(BF16), wd_ref[...])

    halves = []
    for hf in range(2):
        sl = slice(hf * CB, (hf + 1) * CB)
        halves.append(_sigmoid(pg[0 + hf][0].astype(F32)) * y_a[:, sl]
                      + _sigmoid(pg[2 + hf][0].astype(F32)) * y_na[:, sl]
                      + _sigmoid(pg[4 + hf][0].astype(F32)) * y_hy[:, sl]
                      + _sigmoid(pg[6 + hf][0].astype(F32)) * y_cf[:, sl])
    merged = jnp.concatenate(halves, axis=-1).astype(BF16)
    o_ref[0] = x_ref[0] + g1_ref[0] * dot(merged, wo_ref[...])


def _mix(x, g1, px, attn, x0, u, yh, wts, tm):
    b, s, d = x.shape
    row = lambda width: pl.BlockSpec((1, tm, width), lambda bi, i: (bi, i, 0))
    full2 = lambda a: pl.BlockSpec(a.shape, lambda bi, i: (0, 0))
    in_specs = ([row(d), pl.BlockSpec((1, 1, d), lambda bi, i: (bi, 0, 0))]
                + _col_specs(OFF_A, 3 * D_A, tm, s) + _halo_specs(OFF_A, 3 * D_A, tm, s)
                + [row(D_NA), row(D_HY), row(D_HY), row(D_HY)]
                + _col_specs(OFF_CF, 2 * D_CF, tm, s) + _halo_specs(OFF_CF, 2 * D_CF, tm, s)
                + _col_specs(OFF_G, N_BRANCH * D_MODEL, tm, s)
                + [full2(w) for w in wts])
    n_px = 3 + 6 + 2 + 4 + 8
    args = [x, g1] + [px] * 9 + [attn, x0, u, yh] + [px] * 14 + list(wts)
    assert len(args) == len(in_specs) and n_px == 23
    return pl.pallas_call(
        functools.partial(_mix_kernel, tm=tm),
        grid=(b, s // tm),
        in_specs=in_specs,
        out_specs=row(d),
        out_shape=jax.ShapeDtypeStruct((b, s, d), F32),
        compiler_params=_cparams(("parallel", "parallel")),
        name="mixer_out",
    )(*args)


def _ffn_kernel(x_ref, g_ref, sh_ref, sc_ref, gate_ref, wa_ref, wu_ref, wo_ref, o_ref, h_ref, acc_ref):
    f = pl.program_id(2)

    @pl.when(f == 0)
    def _():
        h_ref[...] = _mod_rmsnorm(x_ref[0], g_ref[...], sh_ref[0], sc_ref[0]).astype(BF16)
        acc_ref[...] = jnp.zeros_like(acc_ref)

    h = h_ref[...]
    a = jnp.dot(h, wa_ref[...], preferred_element_type=F32)
    up = jnp.dot(h, wu_ref[...], preferred_element_type=F32)
    act = (_silu(a) * up).astype(BF16)
    acc_ref[...] += jnp.dot(act, wo_ref[...], preferred_element_type=F32)

    @pl.when(f == pl.num_programs(2) - 1)
    def _():
        o_ref[0] = x_ref[0] + gate_ref[0] * acc_ref[...]


def _ffn(x, g, sh, sc, gate, w_in, w_out, tm, tf):
    b, s, d = x.shape
    dff = w_out.shape[0]
    nf = dff // tf
    vec = pl.BlockSpec((1, 1, d), lambda bi, i, f: (bi, 0, 0))
    return pl.pallas_call(
        _ffn_kernel,
        grid=(b, s // tm, nf),
        in_specs=[pl.BlockSpec((1, tm, d), lambda bi, i, f: (bi, i, 0)),
                  pl.BlockSpec((1, d), lambda bi, i, f: (0, 0)),
                  vec, vec, vec,
                  pl.BlockSpec((d, tf), lambda bi, i, f: (0, f)),
                  pl.BlockSpec((d, tf), lambda bi, i, f: (0, f + nf)),
                  pl.BlockSpec((tf, d), lambda bi, i, f: (f, 0))],
        out_specs=pl.BlockSpec((1, tm, d), lambda bi, i, f: (bi, i, 0)),
        out_shape=jax.ShapeDtypeStruct((b, s, d), F32),
        scratch_shapes=[pltpu.VMEM((tm, d), BF16), pltpu.VMEM((tm, d), F32)],
        compiler_params=_cparams(("parallel", "parallel", "arbitrary")),
        name="ffn",
    )(x, g.reshape(1, d), sh, sc, gate, w_in, w_in, w_out)


def _heads(t):
    b, s, _ = t.shape
    return t.reshape(b, s, NA_HEADS, NA_HD).transpose(0, 2, 1, 3)


def _unheads(t):
    b, h, s, hd = t.shape
    return t.transpose(0, 2, 1, 3).reshape(b, s, h * hd)


def kernel(x, c, ctx, c_ctx, w_mod, b_mod, g_norm1, g_norm2, w_in, conv_a, w_a_out, q_gain, k_gain, rpb, w_na_out, conv_h, filt_w1, filt_b1, filt_w2, filt_b2, filt_w3, filt_b3, filt_w4, filt_freq, hy_skip, w_h_out, conv_d, conv_d_b, ln_g, ln_b, w_d_out, w_o, w_ffn_in, w_ffn_out):
    depth = w_mod.shape[0]
    b, s, d = x.shape
    n_ctx = ctx.shape[1]
    cc = jnp.zeros((8, d), F32).at[:b].set(c).at[b].set(c_ctx)

    for l in range(depth):
        last = l == depth - 1
        mods = _modulation(cc, w_mod[l], b_mod[l])
        lat = [mods[:b, t * d:(t + 1) * d].reshape(b, 1, d) for t in range(6)]
        cxm = [jnp.broadcast_to(mods[b:b + 1, t * d:(t + 1) * d].reshape(1, 1, d), (b, 1, d)) for t in range(6)]
        sh1, sc1, g1, sh2, sc2, g2 = lat
        csh1, csc1, cg1, csh2, csc2, cg2 = cxm

        w_in_b = w_in[l].astype(BF16)
        mix_w = (conv_a[l], w_a_out[l].astype(BF16), w_na_out[l].astype(BF16), hy_skip[l].reshape(1, -1),
                 w_h_out[l].astype(BF16), conv_d[l], conv_d_b[l].reshape(1, -1), ln_g[l].reshape(1, -1),
                 ln_b[l].reshape(1, -1), w_d_out[l].astype(BF16), w_o[l].astype(BF16))
        w_ffn_in_b = w_ffn_in[l].astype(BF16)
        w_ffn_out_b = w_ffn_out[l].astype(BF16)
        filt_args = (filt_w1[l], filt_b1[l], filt_w2[l], filt_b2[l], filt_w3[l], filt_b3[l], filt_w4[l], filt_freq[l])

        px = _in_proj(x, g_norm1[l], sh1, sc1, w_in_b, 1024, 2432)
        if last:
            kv_c = _in_proj(ctx, g_norm1[l], csh1, csc1, w_in_b[:, OFF_K:OFF_HY], n_ctx, 2 * D_NA)
            kc_raw, vc_raw = kv_c[..., :D_NA], kv_c[..., D_NA:]
        else:
            pc = _in_proj(ctx, g_norm1[l], csh1, csc1, w_in_b, n_ctx, 2432)
            kc_raw, vc_raw = pc[..., OFF_K:OFF_V], pc[..., OFF_V:OFF_HY]
        kc_h, vc_h = _heads(kc_raw), _heads(vc_raw)

        table = _bias_table(rpb[l])
        attn_x = _unheads(_attn_latent(_heads(px[..., OFF_Q:OFF_K]), _heads(px[..., OFF_K:OFF_V]),
                                       _heads(px[..., OFF_V:OFF_HY]), kc_h, vc_h, q_gain[l], k_gain[l], table))

        x0, u = _hy_pre(px, conv_h[l], 512)
        kern = _conv_kernel_sequence(_hyena_filter(s, *filt_args))
        yh = _long_conv_latent(u, kern)
        x_new = _mix(x, g1, px, attn_x, x0, u, yh, mix_w, 256)

        if not last:
            attn_c = _unheads(_attn_ctx(_heads(pc[..., OFF_Q:OFF_K]), kc_h, vc_h, q_gain[l], k_gain[l]))
            x0c, uc = _hy_pre(pc, conv_h[l], n_ctx)
            kern_c = _conv_kernel_sequence(_hyena_filter(n_ctx, *filt_args))
            yhc = _long_conv_ctx(uc, kern_c)
            ctx = _mix(ctx, cg1, pc, attn_c, x0c, uc, yhc, mix_w, n_ctx)
            ctx = _ffn(ctx, g_norm2[l], csh2, csc2, cg2, w_ffn_in_b, w_ffn_out_b, n_ctx, 1408)

        x = _ffn(x_new, g_norm2[l], sh2, sc2, g2, w_ffn_in_b, w_ffn_out_b, 512, 1408)
    return x
```

```python
import functools
import math

import numpy as np
import jax
import jax.numpy as jnp
from jax import lax
from jax.experimental import pallas as pl
from jax.experimental.pallas import tpu as pltpu

F32 = jnp.float32
BF16 = jnp.bfloat16
HIGHEST = lax.Precision.HIGHEST

D_MODEL = 1024
GRID_W = 64
N_BRANCH = 4
D_A = D_MODEL // 2
SC_K = 3
D_NA = D_MODEL // 2
NA_HEADS = 8
NA_HD = D_NA // NA_HEADS
NA_KH = 8
NA_KW = 16
D_HY = D_MODEL // 2
HY_EMB = 33
HY_FH = 64
HY_SHIFT = 0.05
HY_FAST = 0.3
HY_SLOW = 1.5
HY_TARGET = 1e-2
D_CF = D_MODEL // 2
CF_K = 31
D_FF = ((8 * D_MODEL + 3 * 256 - 1) // (3 * 256)) * 256
EPS = 1e-6
OFF_A = 0
OFF_Q = OFF_A + 3 * D_A
OFF_K = OFF_Q + D_NA
OFF_V = OFF_K + D_NA
OFF_HY = OFF_V + D_NA
OFF_CF = OFF_HY + 3 * D_HY
OFF_G = OFF_CF + 2 * D_CF
N_IN = OFF_G + N_BRANCH * D_MODEL

CB = 512
HALO = 16
NEG = -1e30
VMEM_LIMIT = 56 * 1024 * 1024

QROWS = 8
WROWS = 16
N_ROWS = 64


def _cparams(sem):
    return pltpu.CompilerParams(dimension_semantics=sem, vmem_limit_bytes=VMEM_LIMIT)


def _sigmoid(x):
    return 1.0 / (1.0 + jnp.exp(-x))


def _silu(x):
    return x * _sigmoid(x)


def _rms(x):
    return x * lax.rsqrt(jnp.mean(x * x, axis=-1, keepdims=True) + EPS)


def _mod_rmsnorm(x, g, shift, scale):
    return (_rms(x) * g) * (1.0 + scale) + shift


def _mod_kernel(c_ref, w_ref, b_ref, o_ref):
    s = _silu(c_ref[...])
    o_ref[...] = jnp.dot(s, w_ref[...], preferred_element_type=F32, precision=HIGHEST) + b_ref[...]


def _modulation(cc, w, b):
    m, d = cc.shape
    n = w.shape[1]
    tn = 1536
    return pl.pallas_call(
        _mod_kernel,
        grid=(n // tn,),
        in_specs=[pl.BlockSpec((m, d), lambda j: (0, 0)),
                  pl.BlockSpec((d, tn), lambda j: (0, j)),
                  pl.BlockSpec((1, tn), lambda j: (0, j))],
        out_specs=pl.BlockSpec((m, tn), lambda j: (0, j)),
        out_shape=jax.ShapeDtypeStruct((m, n), F32),
        compiler_params=_cparams(("arbitrary",)),
        name="modulation",
    )(cc, w, b.reshape(1, n))


def _in_proj_kernel(x_ref, g_ref, sh_ref, sc_ref, w_ref, o_ref, h_ref):
    @pl.when(pl.program_id(2) == 0)
    def _():
        h_ref[...] = _mod_rmsnorm(x_ref[0], g_ref[...], sh_ref[0], sc_ref[0]).astype(BF16)

    o_ref[0] = jnp.dot(h_ref[...], w_ref[...], preferred_element_type=F32).astype(o_ref.dtype)


def _in_proj(x, g, sh, sc, w, tm, tn):
    b, s, d = x.shape
    n = w.shape[1]
    return pl.pallas_call(
        _in_proj_kernel,
        grid=(b, s // tm, n // tn),
        in_specs=[pl.BlockSpec((1, tm, d), lambda bi, i, j: (bi, i, 0)),
                  pl.BlockSpec((1, d), lambda bi, i, j: (0, 0)),
                  pl.BlockSpec((1, 1, d), lambda bi, i, j: (bi, 0, 0)),
                  pl.BlockSpec((1, 1, d), lambda bi, i, j: (bi, 0, 0)),
                  pl.BlockSpec((d, tn), lambda bi, i, j: (0, j))],
        out_specs=pl.BlockSpec((1, tm, tn), lambda bi, i, j: (bi, i, j)),
        out_shape=jax.ShapeDtypeStruct((b, s, n), BF16),
        scratch_shapes=[pltpu.VMEM((tm, d), BF16)],
        compiler_params=_cparams(("parallel", "parallel", "arbitrary")),
        name="in_proj",
    )(x, g.reshape(1, d), sh, sc, w)


def _attn_latent_kernel(q_ref, k_ref, v_ref, kc_ref, vc_ref, qg_ref, kg_ref, tab_ref, o_ref):
    j = pl.program_id(2)
    wr = jnp.clip(QROWS * j - (WROWS - QROWS) // 2, 0, N_ROWS - WROWS)
    start = pl.multiple_of(wr * GRID_W, 256)
    nwin = WROWS * GRID_W
    qn = (_rms(q_ref[0, 0].astype(F32)) * qg_ref[...] * (NA_HD ** -0.5)).astype(BF16)
    kn = (_rms(k_ref[0, 0, pl.ds(start, nwin), :].astype(F32)) * kg_ref[...]).astype(BF16)
    vw = v_ref[0, 0, pl.ds(start, nwin), :]
    kcn = (_rms(kc_ref[0, 0].astype(F32)) * kg_ref[...]).astype(BF16)
    dn = (((1,), (1,)), ((), ()))
    s_loc = lax.dot_general(qn, kn, dn, preferred_element_type=F32) + tab_ref[0, 0]
    s_ctx = lax.dot_general(qn, kcn, dn, preferred_element_type=F32)
    m = jnp.maximum(jnp.max(s_loc, axis=-1, keepdims=True), jnp.max(s_ctx, axis=-1, keepdims=True))
    p_loc = jnp.exp(s_loc - m)
    p_ctx = jnp.exp(s_ctx - m)
    den = jnp.sum(p_loc, axis=-1, keepdims=True) + jnp.sum(p_ctx, axis=-1, keepdims=True)
    o = (jnp.dot(p_loc.astype(BF16), vw, preferred_element_type=F32)
         + jnp.dot(p_ctx.astype(BF16), vc_ref[0, 0], preferred_element_type=F32))
    o_ref[0, 0] = (o / den).astype(o_ref.dtype)


def _attn_latent(q, k, v, kc, vc, qg, kg, table):
    b, h, s, hd = q.shape
    c = kc.shape[2]
    tq = QROWS * GRID_W
    nj = s // tq
    nwin = WROWS * GRID_W

    def tab_map(bi, hi, j):
        return (hi, jnp.where(j == 0, 0, jnp.where(j == nj - 1, 2, 1)), 0, 0)

    return pl.pallas_call(
        _attn_latent_kernel,
        grid=(b, h, nj),
        in_specs=[pl.BlockSpec((1, 1, tq, hd), lambda bi, hi, j: (bi, hi, j, 0)),
                  pl.BlockSpec((1, 1, s, hd), lambda bi, hi, j: (bi, hi, 0, 0)),
                  pl.BlockSpec((1, 1, s, hd), lambda bi, hi, j: (bi, hi, 0, 0)),
                  pl.BlockSpec((1, 1, c, hd), lambda bi, hi, j: (bi, hi, 0, 0)),
                  pl.BlockSpec((1, 1, c, hd), lambda bi, hi, j: (bi, hi, 0, 0)),
                  pl.BlockSpec((1, hd), lambda bi, hi, j: (0, 0)),
                  pl.BlockSpec((1, hd), lambda bi, hi, j: (0, 0)),
                  pl.BlockSpec((1, 1, tq, nwin), tab_map)],
        out_specs=pl.BlockSpec((1, 1, tq, hd), lambda bi, hi, j: (bi, hi, j, 0)),
        out_shape=jax.ShapeDtypeStruct((b, h, s, hd), BF16),
        compiler_params=_cparams(("parallel", "parallel", "arbitrary")),
        name="attn_latent",
    )(q, k, v, kc, vc, qg.reshape(1, hd), kg.reshape(1, hd), table)


def _attn_ctx_kernel(q_ref, k_ref, v_ref, qg_ref, kg_ref, o_ref):
    qn = (_rms(q_ref[0, 0].astype(F32)) * qg_ref[...] * (NA_HD ** -0.5)).astype(BF16)
    kn = (_rms(k_ref[0, 0].astype(F32)) * kg_ref[...]).astype(BF16)
    s = lax.dot_general(qn, kn, (((1,), (1,)), ((), ())), preferred_element_type=F32)
    m = jnp.max(s, axis=-1, keepdims=True)
    p = jnp.exp(s - m)
    den = jnp.sum(p, axis=-1, keepdims=True)
    o = jnp.dot(p.astype(BF16), v_ref[0, 0], preferred_element_type=F32)
    o_ref[0, 0] = (o / den).astype(o_ref.dtype)


def _attn_ctx(q, k, v, qg, kg):
    b, h, c, hd = q.shape
    spec = pl.BlockSpec((1, 1, c, hd), lambda bi, hi: (bi, hi, 0, 0))
    gspec = pl.BlockSpec((1, hd), lambda bi, hi: (0, 0))
    return pl.pallas_call(
        _attn_ctx_kernel,
        grid=(b, h),
        in_specs=[spec, spec, spec, gspec, gspec],
        out_specs=spec,
        out_shape=jax.ShapeDtypeStruct((b, h, c, hd), BF16),
        compiler_params=_cparams(("parallel", "parallel")),
        name="attn_ctx",
    )(q, k, v, qg.reshape(1, hd), kg.reshape(1, hd))


N_DR = 2 * NA_KH - 1
N_DC = 2 * NA_KW - 1
DR_SLOTS = 24
DC_PAD = 32


@functools.lru_cache(maxsize=None)
def _bias_constants():
    qc = np.arange(GRID_W)[:, None]
    kc = np.arange(2 * GRID_W)[None, :] % GRID_W
    cs = np.clip(qc - NA_KW // 2, 0, GRID_W - NA_KW)
    ok = (kc >= cs) & (kc < cs + NA_KW)
    dc = np.where(ok, kc - qc + NA_KW - 1, DC_PAD - 1)
    onehot = (np.arange(DC_PAD)[:, None, None] == dc[None]) & ok[None]
    colneg = np.where(ok, 0.0, NEG).reshape(1, -1)
    rowneg = np.full((DR_SLOTS, 1), NEG)
    rowneg[1:N_DR + 1] = 0.0
    f = np.float32
    return onehot.reshape(DC_PAD, -1).astype(f), colneg.astype(f), rowneg.astype(f)


def _bias_cols_kernel(rpb_ref, oh_ref, colneg_ref, rowneg_ref, o_ref):
    sel = jnp.dot(rpb_ref[0], oh_ref[...], preferred_element_type=F32, precision=HIGHEST)
    o_ref[0] = sel + colneg_ref[...] + rowneg_ref[...]


def _bias_table_kernel(m_ref, o_ref):
    left = lax.broadcasted_iota(jnp.int32, (GRID_W, 2 * GRID_W), 1) < GRID_W
    nj = N_ROWS // QROWS
    for ty, j in enumerate((0, 1, nj - 1)):
        wr = min(max(QROWS * j - (WROWS - QROWS) // 2, 0), N_ROWS - WROWS)
        for qr in range(QROWS):
            r = QROWS * j + qr
            rs = min(max(r - NA_KH // 2, 0), N_ROWS - NA_KH)
            slots = [kr - r + NA_KH if rs <= kr < rs + NA_KH else 0 for kr in range(wr, wr + WROWS)]
            for kp in range(WROWS // 2):
                e0, e1 = slots[2 * kp], slots[2 * kp + 1]
                blk = m_ref[0, e0] if e0 == e1 else jnp.where(left, m_ref[0, e0], m_ref[0, e1])
                o_ref[0, ty, qr * GRID_W:(qr + 1) * GRID_W, kp * 2 * GRID_W:(kp + 1) * 2 * GRID_W] = blk


def _bias_table(rpb):
    h = rpb.shape[0]
    onehot, colneg, rowneg = _bias_constants()
    rp = jnp.zeros((h, DR_SLOTS, DC_PAD), F32).at[:, 1:N_DR + 1, :N_DC].set(rpb)
    ncol = GRID_W * 2 * GRID_W
    cols = pl.pallas_call(
        _bias_cols_kernel,
        grid=(h,),
        in_specs=[pl.BlockSpec((1, DR_SLOTS, DC_PAD), lambda hi: (hi, 0, 0)),
                  pl.BlockSpec((DC_PAD, ncol), lambda hi: (0, 0)),
                  pl.BlockSpec((1, ncol), lambda hi: (0, 0)),
                  pl.BlockSpec((DR_SLOTS, 1), lambda hi: (0, 0))],
        out_specs=pl.BlockSpec((1, DR_SLOTS, ncol), lambda hi: (hi, 0, 0)),
        out_shape=jax.ShapeDtypeStruct((h, DR_SLOTS, ncol), F32),
        compiler_params=_cparams(("parallel",)),
        name="bias_cols",
    )(rp, jnp.asarray(onehot), jnp.asarray(colneg), jnp.asarray(rowneg))
    tq, nwin = QROWS * GRID_W, WROWS * GRID_W
    return pl.pallas_call(
        _bias_table_kernel,
        grid=(h,),
        in_specs=[pl.BlockSpec((1, DR_SLOTS, GRID_W, 2 * GRID_W), lambda hi: (hi, 0, 0, 0))],
        out_specs=pl.BlockSpec((1, 3, tq, nwin), lambda hi: (hi, 0, 0, 0)),
        out_shape=jax.ShapeDtypeStruct((h, 3, tq, nwin), F32),
        compiler_params=_cparams(("parallel",)),
        name="bias_table",
    )(cols.reshape(h, DR_SLOTS, GRID_W, 2 * GRID_W))


def _col_specs(off, width, tm, s):
    return [pl.BlockSpec((1, tm, CB), functools.partial(lambda bi, i, cb: (bi, i, cb), cb=off // CB + t))
            for t in range(width // CB)]


def _halo_specs(off, width, tm, s):
    r = tm // HALO
    last = s // HALO - 1
    specs = []
    for t in range(width // CB):
        cb = off // CB + t
        specs.append(pl.BlockSpec((1, HALO, CB),
                                  functools.partial(lambda bi, i, cb: (bi, jnp.maximum(i * r - 1, 0), cb), cb=cb)))
        specs.append(pl.BlockSpec((1, HALO, CB),
                                  functools.partial(lambda bi, i, cb: (bi, jnp.minimum((i + 1) * r, last), cb), cb=cb)))
    return specs


def _extend(main, prev, nxt, first, last):
    prev = jnp.where(first, 0.0, prev.astype(F32))
    nxt = jnp.where(last, 0.0, nxt.astype(F32))
    return jnp.concatenate([prev, main.astype(F32), nxt], axis=0)


def _dwconv_ext(ext, w, tm):
    k = w.shape[0]
    p = (k - 1) // 2
    n = ext.shape[0]
    acc = None
    for t in range(k):
        d = t - p
        sh = ext if d == 0 else pltpu.roll(ext, (-d) % n, 0)
        term = sh * w[t:t + 1, :]
        acc = term if acc is None else acc + term
    return acc[HALO:HALO + tm]


def _hy_pre_kernel(*refs, tm):
    mains = refs[0:3]
    halos = refs[3:9]
    w_ref = refs[9]
    x0_ref, u_ref = refs[10], refs[11]
    i = pl.program_id(1)
    first = i == 0
    last = i == pl.num_programs(1) - 1
    outs = []
    for t in range(3):
        ext = _extend(mains[t][0], halos[2 * t][0], halos[2 * t + 1][0], first, last)
        outs.append(_dwconv_ext(ext, w_ref[:, t * CB:(t + 1) * CB], tm))
    x0_ref[0] = outs[0].astype(x0_ref.dtype)
    u_ref[0] = outs[1] * outs[2]


def _hy_pre(px, conv_h, tm):
    b, s, _ = px.shape
    in_specs = (_col_specs(OFF_HY, 3 * D_HY, tm, s) + _halo_specs(OFF_HY, 3 * D_HY, tm, s)
                + [pl.BlockSpec((SC_K, 3 * D_HY), lambda bi, i: (0, 0))])
    ospec = pl.BlockSpec((1, tm, D_HY), lambda bi, i: (bi, i, 0))
    return pl.pallas_call(
        functools.partial(_hy_pre_kernel, tm=tm),
        grid=(b, s // tm),
        in_specs=in_specs,
        out_specs=[ospec, ospec],
        out_shape=[jax.ShapeDtypeStruct((b, s, D_HY), BF16), jax.ShapeDtypeStruct((b, s, D_HY), F32)],
        compiler_params=_cparams(("parallel", "parallel")),
        name="hyena_pre",
    )(*([px] * 9), conv_h)


@functools.lru_cache(maxsize=None)
def _filter_features(length):
    t = np.linspace(0.0, 1.0, length)[:, None]
    bands = (HY_EMB - 1) // 2
    w = 2.0 * math.pi * np.arange(length)[:, None] / length
    f = np.linspace(1e-4, bands - 1, bands)[None, :]
    z = np.concatenate([t, np.cos(f * w), -np.sin(f * w)], axis=-1)
    pos = np.concatenate([np.arange(length), [0], np.arange(length - 1, 0, -1)])
    zp = np.zeros((2 * length, HY_FH), np.float32)
    zp[:, :HY_EMB] = z[pos]
    zp[:, HY_EMB] = 1.0
    zp[length, HY_EMB] = 0.0
    deltas = np.abs(np.linspace(math.log(HY_TARGET) / HY_SLOW, math.log(HY_TARGET) / HY_FAST, D_HY))
    return zp, np.tile(deltas, 2)[None, :].astype(np.float32)


def _filter_kernel(z_ref, w1_ref, b1_ref, w2_ref, b2_ref, w3_ref, b3_ref, w4_ref, fr_ref, dl_ref, o_ref):
    z = z_ref[...]
    fr = fr_ref[...]
    dot = functools.partial(jnp.dot, preferred_element_type=F32, precision=HIGHEST)
    h = jnp.sin(fr * (dot(z, w1_ref[...]) + b1_ref[...]))
    h = jnp.sin(fr * (dot(h, w2_ref[...]) + b2_ref[...]))
    h = jnp.sin(fr * (dot(h, w3_ref[...]) + b3_ref[...]))
    h = dot(h, w4_ref[...])
    window = jnp.exp(-z[:, 0:1] * dl_ref[...]) + HY_SHIFT
    o_ref[...] = h * window * z[:, HY_EMB:HY_EMB + 1]


def _hyena_conv_kernel(length, w1, b1, w2, b2, w3, b3, w4, freq):
    zp, deltas = _filter_features(length)
    w1p = jnp.zeros((HY_FH, HY_FH), F32).at[:HY_EMB].set(w1)
    tl = min(length, 512)
    half = length // tl
    full = lambda shape: pl.BlockSpec(shape, lambda i: (0, 0))
    return pl.pallas_call(
        _filter_kernel,
        grid=(2 * half,),
        in_specs=[pl.BlockSpec((tl, HY_FH), lambda i: (i, 0)),
                  full((HY_FH, HY_FH)), full((1, HY_FH)), full((HY_FH, HY_FH)), full((1, HY_FH)),
                  full((HY_FH, HY_FH)), full((1, HY_FH)),
                  pl.BlockSpec((HY_FH, D_HY), lambda i: (0, i // half)), full((1, HY_FH)),
                  pl.BlockSpec((1, D_HY), lambda i: (0, i // half))],
        out_specs=pl.BlockSpec((tl, D_HY), lambda i: (i, 0)),
        out_shape=jax.ShapeDtypeStruct((2 * length, D_HY), F32),
        compiler_params=_cparams(("parallel",)),
        name="hyena_filter",
    )(jnp.asarray(zp), w1p, b1.reshape(1, -1), w2, b2.reshape(1, -1), w3, b3.reshape(1, -1), w4,
      freq.reshape(1, -1), jnp.asarray(deltas))


FFT_N1 = 64
FFT_N2 = 128
FFT_N = FFT_N1 * FFT_N2


@functools.lru_cache(maxsize=None)
def _fft_constants():
    n1, n2, n = FFT_N1, FFT_N2, FFT_N
    k1 = np.arange(n1)[:, None]
    t1 = np.arange(n1)[None, :]
    a1 = 2.0 * math.pi * k1 * t1 / n1
    c1, s1 = np.cos(a1), np.sin(a1)
    h = n1 // 2
    w_data = np.zeros((n1, 2, 2 * h))
    w_data[:, 0, :h], w_data[:, 0, h:] = c1[:, :h], s1[:, :h]
    w_data[:, 1, :h], w_data[:, 1, h:] = -s1[:, :h], c1[:, :h]
    w_kern = np.stack([c1, -s1], axis=1)
    w_inv = np.zeros((2, h, n1, 2))
    w_inv[0, :, :, 0], w_inv[0, :, :, 1] = c1[:, :h].T, -s1[:, :h].T
    w_inv[1, :, :, 0], w_inv[1, :, :, 1] = s1[:, :h].T, c1[:, :h].T
    w_inv /= n
    t2 = np.arange(n2)[None, :]
    at = 2.0 * math.pi * k1 * t2 / n
    twr, twi = np.cos(at), -np.sin(at)
    k2 = np.arange(n2)[:, None]
    a2 = 2.0 * math.pi * k2 * t2 / n2
    c2, s2 = np.cos(a2), np.sin(a2)
    w2f = np.block([[c2, s2], [-s2, c2]])
    w2i = np.block([[c2, -s2], [s2, c2]])
    f = np.float32
    return dict(w_data=w_data.reshape(2 * n1, 2 * h).astype(f), w_kern=w_kern.reshape(2 * n1, n1).astype(f),
                w_inv=w_inv.reshape(2 * h, 2 * n1).astype(f), twr=twr[:, :, None].astype(f),
                twi=twi[:, :, None].astype(f), w2f=w2f.astype(f), w2i=w2i.astype(f))


DFT_SPLITS = 1
DFT_STORE = BF16 if DFT_SPLITS == 1 else F32
MID_KB = 4


def _split(x):
    x = x.astype(F32)
    hi = x.astype(BF16)
    if DFT_SPLITS == 1:
        return [hi]
    return [hi, (x - hi.astype(F32)).astype(BF16)]


def _dft_dot(a, b):
    acc = None
    for i, ai in enumerate(_split(a)):
        for j, bj in enumerate(_split(b)):
            if i + j < DFT_SPLITS:
                t = jnp.dot(ai, bj, preferred_element_type=F32)
                acc = t if acc is None else acc + t
    return acc


def _lmm_kernel(w_ref, z_ref, o_ref):
    o_ref[0] = _dft_dot(w_ref[...], z_ref[0]).astype(o_ref.dtype)


def _left_matmul(w, z, tn, out_dtype):
    g, k, n = z.shape
    m = w.shape[0]
    return pl.pallas_call(
        _lmm_kernel,
        grid=(g, n // tn),
        in_specs=[pl.BlockSpec((m, k), lambda gi, j: (0, 0)),
                  pl.BlockSpec((1, k, tn), lambda gi, j: (gi, 0, j))],
        out_specs=pl.BlockSpec((1, m, tn), lambda gi, j: (gi, 0, j)),
        out_shape=jax.ShapeDtypeStruct((g, m, n), out_dtype),
        compiler_params=_cparams(("parallel", "parallel")),
        name="dft_stage1",
    )(w, z)


def _twiddle_fwd(a, twr, twi):
    n2 = FFT_N2
    ar, ai = a[:n2].astype(F32), a[n2:].astype(F32)
    return ar * twr - ai * twi, ar * twi + ai * twr


def _stage2(w_ref, re, im):
    n2 = FFT_N2
    out = _dft_dot(w_ref[:, :n2], re) + _dft_dot(w_ref[:, n2:], im)
    return out[:n2], out[n2:]


def _kspec_kernel(a_ref, twr_ref, twi_ref, w2f_ref, o_ref):
    for t in range(MID_KB):
        br, bi = _twiddle_fwd(a_ref[0, t], twr_ref[t], twi_ref[t])
        xr, xi = _stage2(w2f_ref, br, bi)
        o_ref[t, :FFT_N2, :] = xr
        o_ref[t, FFT_N2:, :] = xi


def _fft_mid_kernel(a_ref, kf_ref, twr_ref, twi_ref, w2f_ref, w2i_ref, o_ref):
    n2 = FFT_N2
    for t in range(MID_KB):
        twr, twi = twr_ref[t], twi_ref[t]
        br, bi = _twiddle_fwd(a_ref[0, t], twr, twi)
        xr, xi = _stage2(w2f_ref, br, bi)
        kr, ki = kf_ref[t, :n2, :], kf_ref[t, n2:, :]
        yr = xr * kr - xi * ki
        yi = xr * ki + xi * kr
        cr, ci = _stage2(w2i_ref, yr, yi)
        o_ref[0, t, :n2, :] = (cr * twr + ci * twi).astype(o_ref.dtype)
        o_ref[0, t, n2:, :] = (ci * twr - cr * twi).astype(o_ref.dtype)


def _long_conv_latent(u, kern):
    b, l, ch = u.shape
    n1, n2 = FFT_N1, FFT_N2
    cst = _fft_constants()
    cols = n2 * ch
    kb = MID_KB
    tw_spec = pl.BlockSpec((kb, n2, 1), lambda k1, p: (k1, 0, 0))
    w2_spec = pl.BlockSpec((2 * n2, 2 * n2), lambda k1, p: (0, 0))
    ak = _left_matmul(jnp.asarray(cst["w_kern"]), kern.reshape(1, n1, cols), 8192, DFT_STORE)
    kf = pl.pallas_call(
        _kspec_kernel,
        grid=(n1 // kb, 1),
        in_specs=[pl.BlockSpec((1, kb, 2 * n2, ch), lambda k1, p: (0, k1, 0, 0)), tw_spec, tw_spec, w2_spec],
        out_specs=pl.BlockSpec((kb, 2 * n2, ch), lambda k1, p: (k1, 0, 0)),
        out_shape=jax.ShapeDtypeStruct((n1, 2 * n2, ch), F32),
        compiler_params=_cparams(("parallel", "arbitrary")),
        name="dft_kernel_spectrum",
    )(ak.reshape(1, n1, 2 * n2, ch), jnp.asarray(cst["twr"]), jnp.asarray(cst["twi"]), jnp.asarray(cst["w2f"]))
    npair = b // 2
    a = _left_matmul(jnp.asarray(cst["w_data"]), u.reshape(npair, n1, cols), 8192, DFT_STORE)
    d = pl.pallas_call(
        _fft_mid_kernel,
        grid=(n1 // kb, npair),
        in_specs=[pl.BlockSpec((1, kb, 2 * n2, ch), lambda k1, p: (p, k1, 0, 0)),
                  pl.BlockSpec((kb, 2 * n2, ch), lambda k1, p: (k1, 0, 0)),
                  tw_spec, tw_spec, w2_spec, w2_spec],
        out_specs=pl.BlockSpec((1, kb, 2 * n2, ch), lambda k1, p: (p, k1, 0, 0)),
        out_shape=jax.ShapeDtypeStruct((npair, n1, 2 * n2, ch), DFT_STORE),
        compiler_params=_cparams(("parallel", "arbitrary")),
        name="dft_mid",
    )(a.reshape(npair, n1, 2 * n2, ch), kf, jnp.asarray(cst["twr"]), jnp.asarray(cst["twi"]),
      jnp.asarray(cst["w2f"]), jnp.asarray(cst["w2i"]))
    y = _left_matmul(jnp.asarray(cst["w_inv"]), d.reshape(npair, 2 * n1, cols), 8192, F32)
    return y.reshape(b, l, ch)


@functools.lru_cache(maxsize=None)
def _dft_constants(n):
    k = np.arange(n)[:, None]
    t = np.arange(n)[None, :]
    ang = 2.0 * math.pi * k * t / n
    return np.cos(ang).astype(np.float32), (-np.sin(ang)).astype(np.float32)


def _conv_ctx_kernel(u_ref, kern_ref, fr_ref, fi_ref, o_ref):
    l = u_ref.shape[1]
    n = 2 * l
    dot = functools.partial(jnp.dot, preferred_element_type=F32, precision=HIGHEST)
    fr, fi = fr_ref[...], fi_ref[...]
    kern = kern_ref[...]
    kr, ki = dot(fr, kern), dot(fi, kern)
    u = u_ref[0]
    ur, ui = dot(fr[:, :l], u), dot(fi[:, :l], u)
    yr = ur * kr - ui * ki
    yi = ur * ki + ui * kr
    o_ref[0] = (dot(fr[:l, :], yr) + dot(fi[:l, :], yi)) * (1.0 / n)


def _long_conv_ctx(u, kern):
    b, l, ch = u.shape
    n = 2 * l
    fr, fi = _dft_constants(n)
    return pl.pallas_call(
        _conv_ctx_kernel,
        grid=(b,),
        in_specs=[pl.BlockSpec((1, l, ch), lambda bi: (bi, 0, 0)),
                  pl.BlockSpec((n, ch), lambda bi: (0, 0)),
                  pl.BlockSpec((n, n), lambda bi: (0, 0)),
                  pl.BlockSpec((n, n), lambda bi: (0, 0))],
        out_specs=pl.BlockSpec((1, l, ch), lambda bi: (bi, 0, 0)),
        out_shape=jax.ShapeDtypeStruct((b, l, ch), F32),
        compiler_params=_cparams(("parallel",)),
        name="long_conv_ctx",
    )(u, kern, jnp.asarray(fr), jnp.asarray(fi))


def _mix_kernel(*refs, tm):
    it = iter(refs)
    x_ref, g1_ref = next(it), next(it)
    pa = [next(it) for _ in range(3)]
    pa_h = [next(it) for _ in range(6)]
    attn_ref, x0_ref, u_ref, yh_ref = next(it), next(it), next(it), next(it)
    pcf = [next(it) for _ in range(2)]
    pcf_h = [next(it) for _ in range(4)]
    pg = [next(it) for _ in range(8)]
    (conva_ref, wa_ref, wna_ref, skip_ref, wh_ref, convd_ref, convdb_ref, lng_ref, lnb_ref, wd_ref,
     wo_ref, o_ref) = [next(it) for _ in range(12)]

    i = pl.program_id(1)
    first = i == 0
    last = i == pl.num_programs(1) - 1
    dot = functools.partial(jnp.dot, preferred_element_type=F32)

    ext_x = _extend(pa[0][0], pa_h[0][0], pa_h[1][0], first, last)
    ext_c = _extend(pa[2][0], pa_h[4][0], pa_h[5][0], first, last)
    za = pa[1][0].astype(F32) * _dwconv_ext(ext_c * ext_x, conva_ref[...], tm)
    y_a = dot(za.astype(BF16), wa_ref[...])
    y_na = dot(attn_ref[0], wna_ref[...])
    u = u_ref[0]
    zh = x0_ref[0].astype(F32) * (yh_ref[0] + u * skip_ref[...])
    y_hy = dot(zh.astype(BF16), wh_ref[...])
    ext_a = _extend(pcf[0][0], pcf_h[0][0], pcf_h[1][0], first, last)
    ext_g = _extend(pcf[1][0], pcf_h[2][0], pcf_h[3][0], first, last)
    uc = _dwconv_ext(ext_a * _sigmoid(ext_g), convd_ref[...], tm) + convdb_ref[...]
    mu = jnp.mean(uc, axis=-1, keepdims=True)
    dv = uc - mu
    var = jnp.mean(dv * dv, axis=-1, keepdims=True)
    zc = _silu(dv * lax.rsqrt(var + EPS) * lng_ref[...] + lnb_ref[...])
    y_cf = dot(zc.astype(BF16), wd_ref[...])

    halves = []
    for hf in range(2):
        sl = slice(hf * CB, (hf + 1) * CB)
        halves.append(_sigmoid(pg[0 + hf][0].astype(F32)) * y_a[:, sl]
                      + _sigmoid(pg[2 + hf][0].astype(F32)) * y_na[:, sl]
                      + _sigmoid(pg[4 + hf][0].astype(F32)) * y_hy[:, sl]
                      + _sigmoid(pg[6 + hf][0].astype(F32)) * y_cf[:, sl])
    merged = jnp.concatenate(halves, axis=-1).astype(BF16)
    o_ref[0] = x_ref[0] + g1_ref[0] * dot(merged, wo_ref[...])


def _mix(x, g1, px, attn, x0, u, yh, wts, tm):
    b, s, d = x.shape
    row = lambda width: pl.BlockSpec((1, tm, width), lambda bi, i: (bi, i, 0))
    full2 = lambda a: pl.BlockSpec(a.shape, lambda bi, i: (0, 0))
    in_specs = ([row(d), pl.BlockSpec((1, 1, d), lambda bi, i: (bi, 0, 0))]
                + _col_specs(OFF_A, 3 * D_A, tm, s) + _halo_specs(OFF_A, 3 * D_A, tm, s)
                + [row(D_NA), row(D_HY), row(D_HY), row(D_HY)]
                + _col_specs(OFF_CF, 2 * D_CF, tm, s) + _halo_specs(OFF_CF, 2 * D_CF, tm, s)
                + _col_specs(OFF_G, N_BRANCH * D_MODEL, tm, s)
                + [full2(w) for w in wts])
    n_px = 3 + 6 + 2 + 4 + 8
    args = [x, g1] + [px] * 9 + [attn, x0, u, yh] + [px] * 14 + list(wts)
    assert len(args) == len(in_specs) and n_px == 23
    return pl.pallas_call(
        functools.partial(_mix_kernel, tm=tm),
        grid=(b, s // tm),
        in_specs=in_specs,
        out_specs=row(d),
        out_shape=jax.ShapeDtypeStruct((b, s, d), F32),
        compiler_params=_cparams(("parallel", "parallel")),
        name="mixer_out",
    )(*args)


def _ffn_kernel(x_ref, g_ref, sh_ref, sc_ref, gate_ref, wa_ref, wu_ref, wo_ref, o_ref, h_ref, acc_ref):
    f = pl.program_id(2)

    @pl.when(f == 0)
    def _():
        h_ref[...] = _mod_rmsnorm(x_ref[0], g_ref[...], sh_ref[0], sc_ref[0]).astype(BF16)
        acc_ref[...] = jnp.zeros_like(acc_ref)

    h = h_ref[...]
    a = jnp.dot(h, wa_ref[...], preferred_element_type=F32)
    up = jnp.dot(h, wu_ref[...], preferred_element_type=F32)
    act = (_silu(a) * up).astype(BF16)
    acc_ref[...] += jnp.dot(act, wo_ref[...], preferred_element_type=F32)

    @pl.when(f == pl.num_programs(2) - 1)
    def _():
        o_ref[0] = x_ref[0] + gate_ref[0] * acc_ref[...]


def _ffn(x, g, sh, sc, gate, w_in, w_out, tm, tf):
    b, s, d = x.shape
    dff = w_out.shape[0]
    nf = dff // tf
    vec = pl.BlockSpec((1, 1, d), lambda bi, i, f: (bi, 0, 0))
    return pl.pallas_call(
        _ffn_kernel,
        grid=(b, s // tm, nf),
        in_specs=[pl.BlockSpec((1, tm, d), lambda bi, i, f: (bi, i, 0)),
                  pl.BlockSpec((1, d), lambda bi, i, f: (0, 0)),
                  vec, vec, vec,
                  pl.BlockSpec((d, tf), lambda bi, i, f: (0, f)),
                  pl.BlockSpec((d, tf), lambda bi, i, f: (0, f + nf)),
                  pl.BlockSpec((tf, d), lambda bi, i, f: (f, 0))],
        out_specs=pl.BlockSpec((1, tm, d), lambda bi, i, f: (bi, i, 0)),
        out_shape=jax.ShapeDtypeStruct((b, s, d), F32),
        scratch_shapes=[pltpu.VMEM((tm, d), BF16), pltpu.VMEM((tm, d), F32)],
        compiler_params=_cparams(("parallel", "parallel", "arbitrary")),
        name="ffn",
    )(x, g.reshape(1, d), sh, sc, gate, w_in, w_in, w_out)


def _heads(t):
    b, s, _ = t.shape
    return t.reshape(b, s, NA_HEADS, NA_HD).transpose(0, 2, 1, 3)


def _unheads(t):
    b, h, s, hd = t.shape
    return t.transpose(0, 2, 1, 3).reshape(b, s, h * hd)


def kernel(x, c, ctx, c_ctx, w_mod, b_mod, g_norm1, g_norm2, w_in, conv_a, w_a_out, q_gain, k_gain, rpb, w_na_out, conv_h, filt_w1, filt_b1, filt_w2, filt_b2, filt_w3, filt_b3, filt_w4, filt_freq, hy_skip, w_h_out, conv_d, conv_d_b, ln_g, ln_b, w_d_out, w_o, w_ffn_in, w_ffn_out):
    depth = w_mod.shape[0]
    b, s, d = x.shape
    n_ctx = ctx.shape[1]
    cc = jnp.zeros((8, d), F32).at[:b].set(c).at[b].set(c_ctx)

    for l in range(depth):
        last = l == depth - 1
        mods = _modulation(cc, w_mod[l], b_mod[l])
        lat = [mods[:b, t * d:(t + 1) * d].reshape(b, 1, d) for t in range(6)]
        cxm = [jnp.broadcast_to(mods[b:b + 1, t * d:(t + 1) * d].reshape(1, 1, d), (b, 1, d)) for t in range(6)]
        sh1, sc1, g1, sh2, sc2, g2 = lat
        csh1, csc1, cg1, csh2, csc2, cg2 = cxm

        w_in_b = w_in[l].astype(BF16)
        mix_w = (conv_a[l], w_a_out[l].astype(BF16), w_na_out[l].astype(BF16), hy_skip[l].reshape(1, -1),
                 w_h_out[l].astype(BF16), conv_d[l], conv_d_b[l].reshape(1, -1), ln_g[l].reshape(1, -1),
                 ln_b[l].reshape(1, -1), w_d_out[l].astype(BF16), w_o[l].astype(BF16))
        w_ffn_in_b = w_ffn_in[l].astype(BF16)
        w_ffn_out_b = w_ffn_out[l].astype(BF16)
        filt_args = (filt_w1[l], filt_b1[l], filt_w2[l], filt_b2[l], filt_w3[l], filt_b3[l], filt_w4[l], filt_freq[l])

        px = _in_proj(x, g_norm1[l], sh1, sc1, w_in_b, 1024, 2432)
        if last:
            kv_c = _in_proj(ctx, g_norm1[l], csh1, csc1, w_in_b[:, OFF_K:OFF_HY], n_ctx, 2 * D_NA)
            kc_raw, vc_raw = kv_c[..., :D_NA], kv_c[..., D_NA:]
        else:
            pc = _in_proj(ctx, g_norm1[l], csh1, csc1, w_in_b, n_ctx, 2432)
            kc_raw, vc_raw = pc[..., OFF_K:OFF_V], pc[..., OFF_V:OFF_HY]
        kc_h, vc_h = _heads(kc_raw), _heads(vc_raw)

        table = _bias_table(rpb[l])
        attn_x = _unheads(_attn_latent(_heads(px[..., OFF_Q:OFF_K]), _heads(px[..., OFF_K:OFF_V]),
                                       _heads(px[..., OFF_V:OFF_HY]), kc_h, vc_h, q_gain[l], k_gain[l], table))

        x0, u = _hy_pre(px, conv_h[l], 512)
        kern = _hyena_conv_kernel(s, *filt_args)
        yh = _long_conv_latent(u, kern)
        x_new = _mix(x, g1, px, attn_x, x0, u, yh, mix_w, 256)

        if not last:
            attn_c = _unheads(_attn_ctx(_heads(pc[..., OFF_Q:OFF_K]), kc_h, vc_h, q_gain[l], k_gain[l]))
            x0c, uc = _hy_pre(pc, conv_h[l], n_ctx)
            kern_c = _hyena_conv_kernel(n_ctx, *filt_args)
            yhc = _long_conv_ctx(uc, kern_c)
            ctx = _mix(ctx, cg1, pc, attn_c, x0c, uc, yhc, mix_w, n_ctx)
            ctx = _ffn(ctx, g_norm2[l], csh2, csc2, cg2, w_ffn_in_b, w_ffn_out_b, n_ctx, 1408)

        x = _ffn(x_new, g_norm2[l], sh2, sc2, g2, w_ffn_in_b, w_ffn_out_b, 512, 1408)
    return x
```

```python
import functools
import math

import numpy as np
import jax
import jax.numpy as jnp
from jax import lax
from jax.experimental import pallas as pl
from jax.experimental.pallas import tpu as pltpu

F32 = jnp.float32
BF16 = jnp.bfloat16
HIGHEST = lax.Precision.HIGHEST

D_MODEL = 1024
GRID_W = 64
N_BRANCH = 4
D_A = D_MODEL // 2
SC_K = 3
D_NA = D_MODEL // 2
NA_HEADS = 8
NA_HD = D_NA // NA_HEADS
NA_KH = 8
NA_KW = 16
D_HY = D_MODEL // 2
HY_EMB = 33
HY_FH = 64
HY_SHIFT = 0.05
HY_FAST = 0.3
HY_SLOW = 1.5
HY_TARGET = 1e-2
D_CF = D_MODEL // 2
CF_K = 31
D_FF = ((8 * D_MODEL + 3 * 256 - 1) // (3 * 256)) * 256
EPS = 1e-6
OFF_A = 0
OFF_Q = OFF_A + 3 * D_A
OFF_K = OFF_Q + D_NA
OFF_V = OFF_K + D_NA
OFF_HY = OFF_V + D_NA
OFF_CF = OFF_HY + 3 * D_HY
OFF_G = OFF_CF + 2 * D_CF
N_IN = OFF_G + N_BRANCH * D_MODEL

CB = 512
HALO = 16
NEG = -1e30
VMEM_LIMIT = 56 * 1024 * 1024

QROWS = 8
WROWS = 16
N_ROWS = 64


def _cparams(sem):
    return pltpu.CompilerParams(dimension_semantics=sem, vmem_limit_bytes=VMEM_LIMIT)


def _sigmoid(x):
    return 1.0 / (1.0 + jnp.exp(-x))


def _silu(x):
    return x * _sigmoid(x)


def _rms(x):
    return x * lax.rsqrt(jnp.mean(x * x, axis=-1, keepdims=True) + EPS)


def _mod_rmsnorm(x, g, shift, scale):
    return (_rms(x) * g) * (1.0 + scale) + shift


def _mod_kernel(c_ref, w_ref, b_ref, o_ref):
    s = _silu(c_ref[...])
    o_ref[...] = jnp.dot(s, w_ref[...], preferred_element_type=F32, precision=HIGHEST) + b_ref[...]


def _modulation(cc, w, b):
    m, d = cc.shape
    n = w.shape[1]
    tn = 1536
    return pl.pallas_call(
        _mod_kernel,
        grid=(n // tn,),
        in_specs=[pl.BlockSpec((m, d), lambda j: (0, 0)),
                  pl.BlockSpec((d, tn), lambda j: (0, j)),
                  pl.BlockSpec((1, tn), lambda j: (0, j))],
        out_specs=pl.BlockSpec((m, tn), lambda j: (0, j)),
        out_shape=jax.ShapeDtypeStruct((m, n), F32),
        compiler_params=_cparams(("arbitrary",)),
        name="modulation",
    )(cc, w, b.reshape(1, n))


def _in_proj_kernel(x_ref, g_ref, sh_ref, sc_ref, w_ref, o_ref, h_ref):
    @pl.when(pl.program_id(2) == 0)
    def _():
        h_ref[...] = _mod_rmsnorm(x_ref[0], g_ref[...], sh_ref[0], sc_ref[0]).astype(BF16)

    o_ref[0] = jnp.dot(h_ref[...], w_ref[...], preferred_element_type=F32).astype(o_ref.dtype)


def _in_proj(x, g, sh, sc, w, tm, tn):
    b, s, d = x.shape
    n = w.shape[1]
    return pl.pallas_call(
        _in_proj_kernel,
        grid=(b, s // tm, n // tn),
        in_specs=[pl.BlockSpec((1, tm, d), lambda bi, i, j: (bi, i, 0)),
                  pl.BlockSpec((1, d), lambda bi, i, j: (0, 0)),
                  pl.BlockSpec((1, 1, d), lambda bi, i, j: (bi, 0, 0)),
                  pl.BlockSpec((1, 1, d), lambda bi, i, j: (bi, 0, 0)),
                  pl.BlockSpec((d, tn), lambda bi, i, j: (0, j))],
        out_specs=pl.BlockSpec((1, tm, tn), lambda bi, i, j: (bi, i, j)),
        out_shape=jax.ShapeDtypeStruct((b, s, n), BF16),
        scratch_shapes=[pltpu.VMEM((tm, d), BF16)],
        compiler_params=_cparams(("parallel", "parallel", "arbitrary")),
        name="in_proj",
    )(x, g.reshape(1, d), sh, sc, w)


ATT_GROUPS = (
    (((0, 8), (56, 8)), ((0, 16), (48, 16))),
    (((8, 16),), ((0, 32),)),
    (((24, 16),), ((16, 32),)),
    (((40, 16),), ((32, 32),)),
)
ATT_QG = 16
ATT_KG = 32
LANES = 2 * NA_HD


def _attn_latent_kernel(q_ref, k_ref, v_ref, kc_ref, vc_ref, qg_ref, kg_ref, tab_ref, o_ref, kn_ref, o_scr):
    j = pl.program_id(2)
    nj = pl.num_programs(2)
    wr = jnp.clip(QROWS * j - (WROWS - QROWS) // 2, 0, N_ROWS - WROWS)
    start = pl.multiple_of(wr * GRID_W, 256)
    ty = jnp.where(j == 0, 0, jnp.where(j == nj - 1, 2, 1))
    nwin = WROWS * GRID_W
    head0 = lax.broadcasted_iota(jnp.int32, (1, LANES), 1) < NA_HD

    def headnorm(x):
        sq = x * x
        s0 = jnp.sum(jnp.where(head0, sq, 0.0), axis=-1, keepdims=True)
        s1 = jnp.sum(jnp.where(head0, 0.0, sq), axis=-1, keepdims=True)
        return x * lax.rsqrt(jnp.where(head0, s0, s1) * (1.0 / NA_HD) + EPS)

    qn = headnorm(q_ref[0].astype(F32)) * qg_ref[...] * (NA_HD ** -0.5)
    kn_ref[...] = (headnorm(k_ref[0, pl.ds(start, nwin), :].astype(F32)) * kg_ref[...]).astype(BF16)
    kcn = (headnorm(kc_ref[0].astype(F32)) * kg_ref[...]).astype(BF16)
    vc2 = vc_ref[0]
    dn = (((1,), (1,)), ((), ()))
    qh = [jnp.where(head0, qn, 0.0), jnp.where(head0, 0.0, qn)]
    s_ctx = [lax.dot_general(q.astype(BF16), kcn, dn, preferred_element_type=F32) for q in qh]
    for g, (qchunks, kchunks) in enumerate(ATT_GROUPS):
        ct = 0 if g == 0 else 1
        qrows = [(qr * GRID_W + c, w) for qr in range(QROWS) for (c, w) in qchunks]
        krows = [(kr * GRID_W + c, w) for kr in range(WROWS) for (c, w) in kchunks]
        kg = jnp.concatenate([kn_ref[a:a + w, :] for a, w in krows], axis=0)
        vg = jnp.concatenate([v_ref[0, pl.ds(pl.multiple_of(start + a, 16), w), :] for a, w in krows], axis=0)
        o_heads = []
        for hh in range(2):
            qg = jnp.concatenate([qh[hh][a:a + w] for a, w in qrows], axis=0).astype(BF16)
            sc = jnp.concatenate([s_ctx[hh][a:a + w] for a, w in qrows], axis=0)
            s = lax.dot_general(qg, kg, dn, preferred_element_type=F32) + tab_ref[hh, ct * 3 + ty]
            m = jnp.maximum(jnp.max(s, axis=-1, keepdims=True), jnp.max(sc, axis=-1, keepdims=True))
            p = jnp.exp(s - m)
            pc = jnp.exp(sc - m)
            den = jnp.sum(p, axis=-1, keepdims=True) + jnp.sum(pc, axis=-1, keepdims=True)
            o = (jnp.dot(p.astype(BF16), vg, preferred_element_type=F32)
                 + jnp.dot(pc.astype(BF16), vc2, preferred_element_type=F32))
            o_heads.append(o / den)
        og = jnp.where(head0, o_heads[0], o_heads[1])
        r = 0
        for a, w in qrows:
            o_scr[a:a + w, :] = og[r:r + w]
            r += w
    o_ref[0] = o_scr[...].astype(o_ref.dtype)


def _attn_latent(px, pcx, kc_cb, vc_cb, qg, kg, table):
    b, s, _ = px.shape
    c = pcx.shape[1]
    tq = QROWS * GRID_W
    nj = s // tq
    nwin = WROWS * GRID_W
    nhp = NA_HEADS // 2
    q_cb, k_cb, v_cb = OFF_Q // LANES, OFF_K // LANES, OFF_V // LANES
    gain = lambda t: jnp.tile(t.reshape(1, NA_HD), (1, 2))
    return pl.pallas_call(
        _attn_latent_kernel,
        grid=(b, nhp, nj),
        in_specs=[pl.BlockSpec((1, tq, LANES), lambda bi, hp, j: (bi, j, q_cb + hp)),
                  pl.BlockSpec((1, s, LANES), lambda bi, hp, j: (bi, 0, k_cb + hp)),
                  pl.BlockSpec((1, s, LANES), lambda bi, hp, j: (bi, 0, v_cb + hp)),
                  pl.BlockSpec((1, c, LANES), lambda bi, hp, j: (bi, 0, kc_cb + hp)),
                  pl.BlockSpec((1, c, LANES), lambda bi, hp, j: (bi, 0, vc_cb + hp)),
                  pl.BlockSpec((1, LANES), lambda bi, hp, j: (0, 0)),
                  pl.BlockSpec((1, LANES), lambda bi, hp, j: (0, 0)),
                  pl.BlockSpec((2, 6, QROWS * ATT_QG, WROWS * ATT_KG), lambda bi, hp, j: (hp, 0, 0, 0))],
        out_specs=pl.BlockSpec((1, tq, LANES), lambda bi, hp, j: (bi, j, hp)),
        out_shape=jax.ShapeDtypeStruct((b, s, D_NA), BF16),
        scratch_shapes=[pltpu.VMEM((nwin, LANES), BF16), pltpu.VMEM((tq, LANES), F32)],
        compiler_params=_cparams(("parallel", "parallel", "arbitrary")),
        name="attn_latent",
    )(px, px, px, pcx, pcx, gain(qg), gain(kg), table)


def _attn_ctx_kernel(q_ref, k_ref, v_ref, qg_ref, kg_ref, o_ref):
    qn = (_rms(q_ref[0, 0].astype(F32)) * qg_ref[...] * (NA_HD ** -0.5)).astype(BF16)
    kn = (_rms(k_ref[0, 0].astype(F32)) * kg_ref[...]).astype(BF16)
    s = lax.dot_general(qn, kn, (((1,), (1,)), ((), ())), preferred_element_type=F32)
    m = jnp.max(s, axis=-1, keepdims=True)
    p = jnp.exp(s - m)
    den = jnp.sum(p, axis=-1, keepdims=True)
    o = jnp.dot(p.astype(BF16), v_ref[0, 0], preferred_element_type=F32)
    o_ref[0, 0] = (o / den).astype(o_ref.dtype)


def _attn_ctx(q, k, v, qg, kg):
    b, h, c, hd = q.shape
    spec = pl.BlockSpec((1, 1, c, hd), lambda bi, hi: (bi, hi, 0, 0))
    gspec = pl.BlockSpec((1, hd), lambda bi, hi: (0, 0))
    return pl.pallas_call(
        _attn_ctx_kernel,
        grid=(b, h),
        in_specs=[spec, spec, spec, gspec, gspec],
        out_specs=spec,
        out_shape=jax.ShapeDtypeStruct((b, h, c, hd), BF16),
        compiler_params=_cparams(("parallel", "parallel")),
        name="attn_ctx",
    )(q, k, v, qg.reshape(1, hd), kg.reshape(1, hd))


N_DR = 2 * NA_KH - 1
N_DC = 2 * NA_KW - 1
DR_SLOTS = 24
DC_PAD = 32


@functools.lru_cache(maxsize=None)
def _bias_constants():
    onehots, oks = [], []
    for qchunks, kchunks in ATT_GROUPS[:2]:
        qc = np.concatenate([np.arange(c, c + w) for c, w in qchunks])[:, None]
        kc = np.tile(np.concatenate([np.arange(c, c + w) for c, w in kchunks]), LANES // ATT_KG)[None, :]
        cs = np.clip(qc - NA_KW // 2, 0, GRID_W - NA_KW)
        ok = (kc >= cs) & (kc < cs + NA_KW)
        dc = np.where(ok, kc - qc + NA_KW - 1, DC_PAD - 1)
        onehots.append((np.arange(DC_PAD)[:, None, None] == dc[None]) & ok[None])
        oks.append(ok)
    onehot = np.stack(onehots, axis=1)
    colneg = np.where(np.stack(oks), 0.0, NEG).reshape(1, -1)
    rowneg = np.full((DR_SLOTS, 1), NEG)
    rowneg[1:N_DR + 1] = 0.0
    f = np.float32
    return onehot.reshape(DC_PAD, -1).astype(f), colneg.astype(f), rowneg.astype(f)


def _bias_cols_kernel(rpb_ref, oh_ref, colneg_ref, rowneg_ref, o_ref):
    sel = jnp.dot(rpb_ref[0], oh_ref[...], preferred_element_type=F32, precision=HIGHEST)
    o_ref[0] = sel + colneg_ref[...] + rowneg_ref[...]


def _bias_table_kernel(m_ref, o_ref):
    lane = lax.broadcasted_iota(jnp.int32, (ATT_QG, LANES), 1)
    nj = N_ROWS // QROWS
    per = LANES // ATT_KG
    for ct in range(2):
        for ty, j in enumerate((0, 1, nj - 1)):
            wr = min(max(QROWS * j - (WROWS - QROWS) // 2, 0), N_ROWS - WROWS)
            for qr in range(QROWS):
                r = QROWS * j + qr
                rs = min(max(r - NA_KH // 2, 0), N_ROWS - NA_KH)
                slots = [kr - r + NA_KH if rs <= kr < rs + NA_KH else 0 for kr in range(wr, wr + WROWS)]
                for kq in range(WROWS // per):
                    e = slots[per * kq:per * (kq + 1)]
                    blk = m_ref[0, e[per - 1], ct]
                    for i in range(per - 2, -1, -1):
                        blk = jnp.where(lane < ATT_KG * (i + 1), m_ref[0, e[i], ct], blk)
                    o_ref[0, ct * 3 + ty, qr * ATT_QG:(qr + 1) * ATT_QG, kq * LANES:(kq + 1) * LANES] = blk


def _bias_table(rpb):
    h = rpb.shape[0]
    onehot, colneg, rowneg = _bias_constants()
    rp = jnp.zeros((h, DR_SLOTS, DC_PAD), F32).at[:, 1:N_DR + 1, :N_DC].set(rpb)
    ncol = 2 * ATT_QG * LANES
    cols = pl.pallas_call(
        _bias_cols_kernel,
        grid=(h,),
        in_specs=[pl.BlockSpec((1, DR_SLOTS, DC_PAD), lambda hi: (hi, 0, 0)),
                  pl.BlockSpec((DC_PAD, ncol), lambda hi: (0, 0)),
                  pl.BlockSpec((1, ncol), lambda hi: (0, 0)),
                  pl.BlockSpec((DR_SLOTS, 1), lambda hi: (0, 0))],
        out_specs=pl.BlockSpec((1, DR_SLOTS, ncol), lambda hi: (hi, 0, 0)),
        out_shape=jax.ShapeDtypeStruct((h, DR_SLOTS, ncol), F32),
        compiler_params=_cparams(("parallel",)),
        name="bias_cols",
    )(rp, jnp.asarray(onehot), jnp.asarray(colneg), jnp.asarray(rowneg))
    tqg, nkg = QROWS * ATT_QG, WROWS * ATT_KG
    return pl.pallas_call(
        _bias_table_kernel,
        grid=(h,),
        in_specs=[pl.BlockSpec((1, DR_SLOTS, 2, ATT_QG, LANES), lambda hi: (hi, 0, 0, 0, 0))],
        out_specs=pl.BlockSpec((1, 6, tqg, nkg), lambda hi: (hi, 0, 0, 0)),
        out_shape=jax.ShapeDtypeStruct((h, 6, tqg, nkg), F32),
        compiler_params=_cparams(("parallel",)),
        name="bias_table",
    )(cols.reshape(h, DR_SLOTS, 2, ATT_QG, LANES))


def _col_specs(off, width, tm, s):
    return [pl.BlockSpec((1, tm, CB), functools.partial(lambda bi, i, cb: (bi, i, cb), cb=off // CB + t))
            for t in range(width // CB)]


def _halo_specs(off, width, tm, s):
    r = tm // HALO
    last = s // HALO - 1
    specs = []
    for t in range(width // CB):
        cb = off // CB + t
        specs.append(pl.BlockSpec((1, HALO, CB),
                                  functools.partial(lambda bi, i, cb: (bi, jnp.maximum(i * r - 1, 0), cb), cb=cb)))
        specs.append(pl.BlockSpec((1, HALO, CB),
                                  functools.partial(lambda bi, i, cb: (bi, jnp.minimum((i + 1) * r, last), cb), cb=cb)))
    return specs


def _extend(main, prev, nxt, first, last):
    prev = jnp.where(first, 0.0, prev.astype(F32))
    nxt = jnp.where(last, 0.0, nxt.astype(F32))
    return jnp.concatenate([prev, main.astype(F32), nxt], axis=0)


def _dwconv_ext(ext, w, tm):
    k = w.shape[0]
    p = (k - 1) // 2
    n = ext.shape[0]
    acc = None
    for t in range(k):
        d = t - p
        sh = ext if d == 0 else pltpu.roll(ext, (-d) % n, 0)
        term = sh * w[t:t + 1, :]
        acc = term if acc is None else acc + term
    return acc[HALO:HALO + tm]


def _hy_pre_kernel(*refs, tm):
    mains = refs[0:3]
    halos = refs[3:9]
    w_ref = refs[9]
    x0_ref, u_ref = refs[10], refs[11]
    i = pl.program_id(1)
    first = i == 0
    last = i == pl.num_programs(1) - 1
    outs = []
    for t in range(3):
        ext = _extend(mains[t][0], halos[2 * t][0], halos[2 * t + 1][0], first, last)
        outs.append(_dwconv_ext(ext, w_ref[:, t * CB:(t + 1) * CB], tm))
    x0_ref[0] = outs[0].astype(x0_ref.dtype)
    u_ref[0] = outs[1] * outs[2]


def _hy_pre(px, conv_h, tm):
    b, s, _ = px.shape
    in_specs = (_col_specs(OFF_HY, 3 * D_HY, tm, s) + _halo_specs(OFF_HY, 3 * D_HY, tm, s)
                + [pl.BlockSpec((SC_K, 3 * D_HY), lambda bi, i: (0, 0))])
    ospec = pl.BlockSpec((1, tm, D_HY), lambda bi, i: (bi, i, 0))
    return pl.pallas_call(
        functools.partial(_hy_pre_kernel, tm=tm),
        grid=(b, s // tm),
        in_specs=in_specs,
        out_specs=[ospec, ospec],
        out_shape=[jax.ShapeDtypeStruct((b, s, D_HY), BF16), jax.ShapeDtypeStruct((b, s, D_HY), F32)],
        compiler_params=_cparams(("parallel", "parallel")),
        name="hyena_pre",
    )(*([px] * 9), conv_h)


@functools.lru_cache(maxsize=None)
def _filter_features(length):
    t = np.linspace(0.0, 1.0, length)[:, None]
    bands = (HY_EMB - 1) // 2
    w = 2.0 * math.pi * np.arange(length)[:, None] / length
    f = np.linspace(1e-4, bands - 1, bands)[None, :]
    z = np.concatenate([t, np.cos(f * w), -np.sin(f * w)], axis=-1)
    pos = np.concatenate([np.arange(length), [0], np.arange(length - 1, 0, -1)])
    zp = np.zeros((2 * length, HY_FH), np.float32)
    zp[:, :HY_EMB] = z[pos]
    zp[:, HY_EMB] = 1.0
    zp[length, HY_EMB] = 0.0
    deltas = np.abs(np.linspace(math.log(HY_TARGET) / HY_SLOW, math.log(HY_TARGET) / HY_FAST, D_HY))
    return zp, np.tile(deltas, 2)[None, :].astype(np.float32)


def _filter_kernel(z_ref, w1_ref, b1_ref, w2_ref, b2_ref, w3_ref, b3_ref, w4_ref, fr_ref, dl_ref, o_ref):
    z = z_ref[...]
    fr = fr_ref[...]
    dot = functools.partial(jnp.dot, preferred_element_type=F32, precision=HIGHEST)
    h = jnp.sin(fr * (dot(z, w1_ref[...]) + b1_ref[...]))
    h = jnp.sin(fr * (dot(h, w2_ref[...]) + b2_ref[...]))
    h = jnp.sin(fr * (dot(h, w3_ref[...]) + b3_ref[...]))
    h = dot(h, w4_ref[...])
    window = jnp.exp(-z[:, 0:1] * dl_ref[...]) + HY_SHIFT
    o_ref[...] = h * window * z[:, HY_EMB:HY_EMB + 1]


def _hyena_conv_kernel(length, w1, b1, w2, b2, w3, b3, w4, freq):
    zp, deltas = _filter_features(length)
    w1p = jnp.zeros((HY_FH, HY_FH), F32).at[:HY_EMB].set(w1)
    tl = min(length, 512)
    half = length // tl
    full = lambda shape: pl.BlockSpec(shape, lambda i: (0, 0))
    return pl.pallas_call(
        _filter_kernel,
        grid=(2 * half,),
        in_specs=[pl.BlockSpec((tl, HY_FH), lambda i: (i, 0)),
                  full((HY_FH, HY_FH)), full((1, HY_FH)), full((HY_FH, HY_FH)), full((1, HY_FH)),
                  full((HY_FH, HY_FH)), full((1, HY_FH)),
                  pl.BlockSpec((HY_FH, D_HY), lambda i: (0, i // half)), full((1, HY_FH)),
                  pl.BlockSpec((1, D_HY), lambda i: (0, i // half))],
        out_specs=pl.BlockSpec((tl, D_HY), lambda i: (i, 0)),
        out_shape=jax.ShapeDtypeStruct((2 * length, D_HY), F32),
        compiler_params=_cparams(("parallel",)),
        name="hyena_filter",
    )(jnp.asarray(zp), w1p, b1.reshape(1, -1), w2, b2.reshape(1, -1), w3, b3.reshape(1, -1), w4,
      freq.reshape(1, -1), jnp.asarray(deltas))


FFT_N1 = 64
FFT_N2 = 128
FFT_N = FFT_N1 * FFT_N2


@functools.lru_cache(maxsize=None)
def _fft_constants():
    n1, n2, n = FFT_N1, FFT_N2, FFT_N
    k1 = np.arange(n1)[:, None]
    t1 = np.arange(n1)[None, :]
    a1 = 2.0 * math.pi * k1 * t1 / n1
    c1, s1 = np.cos(a1), np.sin(a1)
    h = n1 // 2
    w_data = np.zeros((n1, 2, 2 * h))
    w_data[:, 0, :h], w_data[:, 0, h:] = c1[:, :h], s1[:, :h]
    w_data[:, 1, :h], w_data[:, 1, h:] = -s1[:, :h], c1[:, :h]
    w_kern = np.stack([c1, -s1], axis=1)
    w_inv = np.zeros((2, h, n1, 2))
    w_inv[0, :, :, 0], w_inv[0, :, :, 1] = c1[:, :h].T, -s1[:, :h].T
    w_inv[1, :, :, 0], w_inv[1, :, :, 1] = s1[:, :h].T, c1[:, :h].T
    w_inv /= n
    t2 = np.arange(n2)[None, :]
    at = 2.0 * math.pi * k1 * t2 / n
    twr, twi = np.cos(at), -np.sin(at)
    k2 = np.arange(n2)[:, None]
    a2 = 2.0 * math.pi * k2 * t2 / n2
    c2, s2 = np.cos(a2), np.sin(a2)
    w2f = np.block([[c2, s2], [-s2, c2]])
    w2i = np.block([[c2, -s2], [s2, c2]])
    f = np.float32
    return dict(w_data=w_data.reshape(2 * n1, 2 * h).astype(f), w_kern=w_kern.reshape(2 * n1, n1).astype(f),
                w_inv=w_inv.reshape(2 * h, 2 * n1).astype(f), twr=twr[:, :, None].astype(f),
                twi=twi[:, :, None].astype(f), w2f=w2f.astype(f), w2i=w2i.astype(f))


DFT_SPLITS = 1
DFT_STORE = BF16 if DFT_SPLITS == 1 else F32
MID_KB = 4


def _split(x):
    x = x.astype(F32)
    hi = x.astype(BF16)
    if DFT_SPLITS == 1:
        return [hi]
    return [hi, (x - hi.astype(F32)).astype(BF16)]


def _dft_dot(a, b):
    acc = None
    for i, ai in enumerate(_split(a)):
        for j, bj in enumerate(_split(b)):
            if i + j < DFT_SPLITS:
                t = jnp.dot(ai, bj, preferred_element_type=F32)
                acc = t if acc is None else acc + t
    return acc


def _lmm_kernel(w_ref, z_ref, o_ref):
    o_ref[0] = _dft_dot(w_ref[...], z_ref[0]).astype(o_ref.dtype)


def _left_matmul(w, z, tn, out_dtype):
    g, k, n = z.shape
    m = w.shape[0]
    return pl.pallas_call(
        _lmm_kernel,
        grid=(g, n // tn),
        in_specs=[pl.BlockSpec((m, k), lambda gi, j: (0, 0)),
                  pl.BlockSpec((1, k, tn), lambda gi, j: (gi, 0, j))],
        out_specs=pl.BlockSpec((1, m, tn), lambda gi, j: (gi, 0, j)),
        out_shape=jax.ShapeDtypeStruct((g, m, n), out_dtype),
        compiler_params=_cparams(("parallel", "parallel")),
        name="dft_stage1",
    )(w, z)


def _twiddle_fwd(a, twr, twi):
    n2 = FFT_N2
    ar, ai = a[:n2].astype(F32), a[n2:].astype(F32)
    return ar * twr - ai * twi, ar * twi + ai * twr


def _stage2(w_ref, re, im):
    n2 = FFT_N2
    out = _dft_dot(w_ref[:, :n2], re) + _dft_dot(w_ref[:, n2:], im)
    return out[:n2], out[n2:]


def _kspec_kernel(a_ref, twr_ref, twi_ref, w2f_ref, o_ref):
    for t in range(MID_KB):
        br, bi = _twiddle_fwd(a_ref[0, t], twr_ref[t], twi_ref[t])
        xr, xi = _stage2(w2f_ref, br, bi)
        o_ref[t, :FFT_N2, :] = xr
        o_ref[t, FFT_N2:, :] = xi


def _fft_mid_kernel(a_ref, kf_ref, twr_ref, twi_ref, w2f_ref, w2i_ref, o_ref):
    n2 = FFT_N2
    for t in range(MID_KB):
        twr, twi = twr_ref[t], twi_ref[t]
        br, bi = _twiddle_fwd(a_ref[0, t], twr, twi)
        xr, xi = _stage2(w2f_ref, br, bi)
        kr, ki = kf_ref[t, :n2, :], kf_ref[t, n2:, :]
        yr = xr * kr - xi * ki
        yi = xr * ki + xi * kr
        cr, ci = _stage2(w2i_ref, yr, yi)
        o_ref[0, t, :n2, :] = (cr * twr + ci * twi).astype(o_ref.dtype)
        o_ref[0, t, n2:, :] = (ci * twr - cr * twi).astype(o_ref.dtype)


def _long_conv_latent(u, kern):
    b, l, ch = u.shape
    n1, n2 = FFT_N1, FFT_N2
    cst = _fft_constants()
    cols = n2 * ch
    kb = MID_KB
    tw_spec = pl.BlockSpec((kb, n2, 1), lambda k1, p: (k1, 0, 0))
    w2_spec = pl.BlockSpec((2 * n2, 2 * n2), lambda k1, p: (0, 0))
    ak = _left_matmul(jnp.asarray(cst["w_kern"]), kern.reshape(1, n1, cols), 8192, DFT_STORE)
    kf = pl.pallas_call(
        _kspec_kernel,
        grid=(n1 // kb, 1),
        in_specs=[pl.BlockSpec((1, kb, 2 * n2, ch), lambda k1, p: (0, k1, 0, 0)), tw_spec, tw_spec, w2_spec],
        out_specs=pl.BlockSpec((kb, 2 * n2, ch), lambda k1, p: (k1, 0, 0)),
        out_shape=jax.ShapeDtypeStruct((n1, 2 * n2, ch), F32),
        compiler_params=_cparams(("parallel", "arbitrary")),
        name="dft_kernel_spectrum",
    )(ak.reshape(1, n1, 2 * n2, ch), jnp.asarray(cst["twr"]), jnp.asarray(cst["twi"]), jnp.asarray(cst["w2f"]))
    npair = b // 2
    a = _left_matmul(jnp.asarray(cst["w_data"]), u.reshape(npair, n1, cols), 8192, DFT_STORE)
    d = pl.pallas_call(
        _fft_mid_kernel,
        grid=(n1 // kb, npair),
        in_specs=[pl.BlockSpec((1, kb, 2 * n2, ch), lambda k1, p: (p, k1, 0, 0)),
                  pl.BlockSpec((kb, 2 * n2, ch), lambda k1, p: (k1, 0, 0)),
                  tw_spec, tw_spec, w2_spec, w2_spec],
        out_specs=pl.BlockSpec((1, kb, 2 * n2, ch), lambda k1, p: (p, k1, 0, 0)),
        out_shape=jax.ShapeDtypeStruct((npair, n1, 2 * n2, ch), DFT_STORE),
        compiler_params=_cparams(("parallel", "arbitrary")),
        name="dft_mid",
    )(a.reshape(npair, n1, 2 * n2, ch), kf, jnp.asarray(cst["twr"]), jnp.asarray(cst["twi"]),
      jnp.asarray(cst["w2f"]), jnp.asarray(cst["w2i"]))
    y = _left_matmul(jnp.asarray(cst["w_inv"]), d.reshape(npair, 2 * n1, cols), 8192, F32)
    return y.reshape(b, l, ch)


@functools.lru_cache(maxsize=None)
def _dft_constants(n):
    k = np.arange(n)[:, None]
    t = np.arange(n)[None, :]
    ang = 2.0 * math.pi * k * t / n
    return np.cos(ang).astype(np.float32), (-np.sin(ang)).astype(np.float32)


def _conv_ctx_kernel(u_ref, kern_ref, fr_ref, fi_ref, o_ref):
    l = u_ref.shape[1]
    n = 2 * l
    dot = functools.partial(jnp.dot, preferred_element_type=F32, precision=HIGHEST)
    fr, fi = fr_ref[...], fi_ref[...]
    kern = kern_ref[...]
    kr, ki = dot(fr, kern), dot(fi, kern)
    u = u_ref[0]
    ur, ui = dot(fr[:, :l], u), dot(fi[:, :l], u)
    yr = ur * kr - ui * ki
    yi = ur * ki + ui * kr
    o_ref[0] = (dot(fr[:l, :], yr) + dot(fi[:l, :], yi)) * (1.0 / n)


def _long_conv_ctx(u, kern):
    b, l, ch = u.shape
    n = 2 * l
    fr, fi = _dft_constants(n)
    return pl.pallas_call(
        _conv_ctx_kernel,
        grid=(b,),
        in_specs=[pl.BlockSpec((1, l, ch), lambda bi: (bi, 0, 0)),
                  pl.BlockSpec((n, ch), lambda bi: (0, 0)),
                  pl.BlockSpec((n, n), lambda bi: (0, 0)),
                  pl.BlockSpec((n, n), lambda bi: (0, 0))],
        out_specs=pl.BlockSpec((1, l, ch), lambda bi: (bi, 0, 0)),
        out_shape=jax.ShapeDtypeStruct((b, l, ch), F32),
        compiler_params=_cparams(("parallel",)),
        name="long_conv_ctx",
    )(u, kern, jnp.asarray(fr), jnp.asarray(fi))


def _mix_kernel(*refs, tm):
    it = iter(refs)
    x_ref, g1_ref = next(it), next(it)
    pa = [next(it) for _ in range(3)]
    pa_h = [next(it) for _ in range(6)]
    attn_ref, x0_ref, u_ref, yh_ref = next(it), next(it), next(it), next(it)
    pcf = [next(it) for _ in range(2)]
    pcf_h = [next(it) for _ in range(4)]
    pg = [next(it) for _ in range(8)]
    (conva_ref, wa_ref, wna_ref, skip_ref, wh_ref, convd_ref, convdb_ref, lng_ref, lnb_ref, wd_ref,
     wo_ref, o_ref) = [next(it) for _ in range(12)]

    i = pl.program_id(1)
    first = i == 0
    last = i == pl.num_programs(1) - 1
    dot = functools.partial(jnp.dot, preferred_element_type=F32)

    ext_x = _extend(pa[0][0], pa_h[0][0], pa_h[1][0], first, last)
    ext_c = _extend(pa[2][0], pa_h[4][0], pa_h[5][0], first, last)
    za = pa[1][0].astype(F32) * _dwconv_ext(ext_c * ext_x, conva_ref[...], tm)
    y_a = dot(za.astype(BF16), wa_ref[...])
    y_na = dot(attn_ref[0], wna_ref[...])
    u = u_ref[0]
    zh = x0_ref[0].astype(F32) * (yh_ref[0] + u * skip_ref[...])
    y_hy = dot(zh.astype(BF16), wh_ref[...])
    ext_a = _extend(pcf[0][0], pcf_h[0][0], pcf_h[1][0], first, last)
    ext_g = _extend(pcf[1][0], pcf_h[2][0], pcf_h[3][0], first, last)
    uc = _dwconv_ext(ext_a * _sigmoid(ext_g), convd_ref[...], tm) + convdb_ref[...]
    mu = jnp.mean(uc, axis=-1, keepdims=True)
    dv = uc - mu
    var = jnp.mean(dv * dv, axis=-1, keepdims=True)
    zc = _silu(dv * lax.rsqrt(var + EPS) * lng_ref[...] + lnb_ref[...])
    y_cf = dot(zc.astype(BF16), wd_ref[...])

    halves = []
    for hf in range(2):
        sl = slice(hf * CB, (hf + 1) * CB)
        halves.append(_sigmoid(pg[0 + hf][0].astype(F32)) * y_a[:, sl]
                      + _sigmoid(pg[2 + hf][0].astype(F32)) * y_na[:, sl]
                      + _sigmoid(pg[4 + hf][0].astype(F32)) * y_hy[:, sl]
                      + _sigmoid(pg[6 + hf][0].astype(F32)) * y_cf[:, sl])
    merged = jnp.concatenate(halves, axis=-1).astype(BF16)
    o_ref[0] = x_ref[0] + g1_ref[0] * dot(merged, wo_ref[...])


def _mix(x, g1, px, attn, x0, u, yh, wts, tm):
    b, s, d = x.shape
    row = lambda width: pl.BlockSpec((1, tm, width), lambda bi, i: (bi, i, 0))
    full2 = lambda a: pl.BlockSpec(a.shape, lambda bi, i: (0, 0))
    in_specs = ([row(d), pl.BlockSpec((1, 1, d), lambda bi, i: (bi, 0, 0))]
                + _col_specs(OFF_A, 3 * D_A, tm, s) + _halo_specs(OFF_A, 3 * D_A, tm, s)
                + [row(D_NA), row(D_HY), row(D_HY), row(D_HY)]
                + _col_specs(OFF_CF, 2 * D_CF, tm, s) + _halo_specs(OFF_CF, 2 * D_CF, tm, s)
                + _col_specs(OFF_G, N_BRANCH * D_MODEL, tm, s)
                + [full2(w) for w in wts])
    n_px = 3 + 6 + 2 + 4 + 8
    args = [x, g1] + [px] * 9 + [attn, x0, u, yh] + [px] * 14 + list(wts)
    assert len(args) == len(in_specs) and n_px == 23
    return pl.pallas_call(
        functools.partial(_mix_kernel, tm=tm),
        grid=(b, s // tm),
        in_specs=in_specs,
        out_specs=row(d),
        out_shape=jax.ShapeDtypeStruct((b, s, d), F32),
        compiler_params=_cparams(("parallel", "parallel")),
        name="mixer_out",
    )(*args)


def _ffn_kernel(x_ref, g_ref, sh_ref, sc_ref, gate_ref, wa_ref, wu_ref, wo_ref, o_ref, h_ref, acc_ref):
    f = pl.program_id(2)

    @pl.when(f == 0)
    def _():
        h_ref[...] = _mod_rmsnorm(x_ref[0], g_ref[...], sh_ref[0], sc_ref[0]).astype(BF16)
        acc_ref[...] = jnp.zeros_like(acc_ref)

    h = h_ref[...]
    a = jnp.dot(h, wa_ref[...], preferred_element_type=F32)
    up = jnp.dot(h, wu_ref[...], preferred_element_type=F32)
    act = (_silu(a) * up).astype(BF16)
    acc_ref[...] += jnp.dot(act, wo_ref[...], preferred_element_type=F32)

    @pl.when(f == pl.num_programs(2) - 1)
    def _():
        o_ref[0] = x_ref[0] + gate_ref[0] * acc_ref[...]


def _ffn(x, g, sh, sc, gate, w_in, w_out, tm, tf):
    b, s, d = x.shape
    dff = w_out.shape[0]
    nf = dff // tf
    vec = pl.BlockSpec((1, 1, d), lambda bi, i, f: (bi, 0, 0))
    return pl.pallas_call(
        _ffn_kernel,
        grid=(b, s // tm, nf),
        in_specs=[pl.BlockSpec((1, tm, d), lambda bi, i, f: (bi, i, 0)),
                  pl.BlockSpec((1, d), lambda bi, i, f: (0, 0)),
                  vec, vec, vec,
                  pl.BlockSpec((d, tf), lambda bi, i, f: (0, f)),
                  pl.BlockSpec((d, tf), lambda bi, i, f: (0, f + nf)),
                  pl.BlockSpec((tf, d), lambda bi, i, f: (f, 0))],
        out_specs=pl.BlockSpec((1, tm, d), lambda bi, i, f: (bi, i, 0)),
        out_shape=jax.ShapeDtypeStruct((b, s, d), F32),
        scratch_shapes=[pltpu.VMEM((tm, d), BF16), pltpu.VMEM((tm, d), F32)],
        compiler_params=_cparams(("parallel", "parallel", "arbitrary")),
        name="ffn",
    )(x, g.reshape(1, d), sh, sc, gate, w_in, w_in, w_out)


def _heads(t):
    b, s, _ = t.shape
    return t.reshape(b, s, NA_HEADS, NA_HD).transpose(0, 2, 1, 3)


def _unheads(t):
    b, h, s, hd = t.shape
    return t.transpose(0, 2, 1, 3).reshape(b, s, h * hd)


def kernel(x, c, ctx, c_ctx, w_mod, b_mod, g_norm1, g_norm2, w_in, conv_a, w_a_out, q_gain, k_gain, rpb, w_na_out, conv_h, filt_w1, filt_b1, filt_w2, filt_b2, filt_w3, filt_b3, filt_w4, filt_freq, hy_skip, w_h_out, conv_d, conv_d_b, ln_g, ln_b, w_d_out, w_o, w_ffn_in, w_ffn_out):
    depth = w_mod.shape[0]
    b, s, d = x.shape
    n_ctx = ctx.shape[1]
    cc = jnp.zeros((8, d), F32).at[:b].set(c).at[b].set(c_ctx)

    for l in range(depth):
        last = l == depth - 1
        mods = _modulation(cc, w_mod[l], b_mod[l])
        lat = [mods[:b, t * d:(t + 1) * d].reshape(b, 1, d) for t in range(6)]
        cxm = [jnp.broadcast_to(mods[b:b + 1, t * d:(t + 1) * d].reshape(1, 1, d), (b, 1, d)) for t in range(6)]
        sh1, sc1, g1, sh2, sc2, g2 = lat
        csh1, csc1, cg1, csh2, csc2, cg2 = cxm

        w_in_b = w_in[l].astype(BF16)
        mix_w = (conv_a[l], w_a_out[l].astype(BF16), w_na_out[l].astype(BF16), hy_skip[l].reshape(1, -1),
                 w_h_out[l].astype(BF16), conv_d[l], conv_d_b[l].reshape(1, -1), ln_g[l].reshape(1, -1),
                 ln_b[l].reshape(1, -1), w_d_out[l].astype(BF16), w_o[l].astype(BF16))
        w_ffn_in_b = w_ffn_in[l].astype(BF16)
        w_ffn_out_b = w_ffn_out[l].astype(BF16)
        filt_args = (filt_w1[l], filt_b1[l], filt_w2[l], filt_b2[l], filt_w3[l], filt_b3[l], filt_w4[l], filt_freq[l])

        px = _in_proj(x, g_norm1[l], sh1, sc1, w_in_b, 1024, 2432)
        if last:
            pcx = _in_proj(ctx, g_norm1[l], csh1, csc1, w_in_b[:, OFF_K:OFF_HY], n_ctx, 2 * D_NA)
            kc_cb, vc_cb = 0, D_NA // LANES
        else:
            pc = pcx = _in_proj(ctx, g_norm1[l], csh1, csc1, w_in_b, n_ctx, 2432)
            kc_cb, vc_cb = OFF_K // LANES, OFF_V // LANES

        table = _bias_table(rpb[l])
        attn_x = _attn_latent(px, pcx, kc_cb, vc_cb, q_gain[l], k_gain[l], table)

        x0, u = _hy_pre(px, conv_h[l], 512)
        kern = _hyena_conv_kernel(s, *filt_args)
        yh = _long_conv_latent(u, kern)
        x_new = _mix(x, g1, px, attn_x, x0, u, yh, mix_w, 256)

        if not last:
            attn_c = _unheads(_attn_ctx(_heads(pc[..., OFF_Q:OFF_K]), _heads(pc[..., OFF_K:OFF_V]),
                                        _heads(pc[..., OFF_V:OFF_HY]), q_gain[l], k_gain[l]))
            x0c, uc = _hy_pre(pc, conv_h[l], n_ctx)
            kern_c = _hyena_conv_kernel(n_ctx, *filt_args)
            yhc = _long_conv_ctx(uc, kern_c)
            ctx = _mix(ctx, cg1, pc, attn_c, x0c, uc, yhc, mix_w, n_ctx)
            ctx = _ffn(ctx, g_norm2[l], csh2, csc2, cg2, w_ffn_in_b, w_ffn_out_b, n_ctx, 1408)

        x = _ffn(x_new, g_norm2[l], sh2, sc2, g2, w_ffn_in_b, w_ffn_out_b, 512, 1408)
    return x
```

```python
import functools
import math

import numpy as np
import jax
import jax.numpy as jnp
from jax import lax
from jax.experimental import pallas as pl
from jax.experimental.pallas import tpu as pltpu

F32 = jnp.float32
BF16 = jnp.bfloat16
HIGHEST = lax.Precision.HIGHEST

D_MODEL = 1024
GRID_W = 64
N_BRANCH = 4
D_A = D_MODEL // 2
SC_K = 3
D_NA = D_MODEL // 2
NA_HEADS = 8
NA_HD = D_NA // NA_HEADS
NA_KH = 8
NA_KW = 16
D_HY = D_MODEL // 2
HY_EMB = 33
HY_FH = 64
HY_SHIFT = 0.05
HY_FAST = 0.3
HY_SLOW = 1.5
HY_TARGET = 1e-2
D_CF = D_MODEL // 2
CF_K = 31
D_FF = ((8 * D_MODEL + 3 * 256 - 1) // (3 * 256)) * 256
EPS = 1e-6
OFF_A = 0
OFF_Q = OFF_A + 3 * D_A
OFF_K = OFF_Q + D_NA
OFF_V = OFF_K + D_NA
OFF_HY = OFF_V + D_NA
OFF_CF = OFF_HY + 3 * D_HY
OFF_G = OFF_CF + 2 * D_CF
N_IN = OFF_G + N_BRANCH * D_MODEL

CB = 512
HALO = 16
NEG = -1e30
VMEM_LIMIT = 56 * 1024 * 1024

QROWS = 8
WROWS = 16
N_ROWS = 64


def _cparams(sem):
    return pltpu.CompilerParams(dimension_semantics=sem, vmem_limit_bytes=VMEM_LIMIT)


def _sigmoid(x):
    return 1.0 / (1.0 + jnp.exp(-x))


def _silu(x):
    return x * _sigmoid(x)


def _rms(x):
    return x * lax.rsqrt(jnp.mean(x * x, axis=-1, keepdims=True) + EPS)


def _mod_rmsnorm(x, g, shift, scale):
    return (_rms(x) * g) * (1.0 + scale) + shift


def _mod_kernel(c_ref, w_ref, b_ref, o_ref):
    s = _silu(c_ref[...])
    o_ref[...] = jnp.dot(s, w_ref[...], preferred_element_type=F32, precision=HIGHEST) + b_ref[...]


def _modulation(cc, w, b):
    m, d = cc.shape
    n = w.shape[1]
    tn = 1536
    return pl.pallas_call(
        _mod_kernel,
        grid=(n // tn,),
        in_specs=[pl.BlockSpec((m, d), lambda j: (0, 0)),
                  pl.BlockSpec((d, tn), lambda j: (0, j)),
                  pl.BlockSpec((1, tn), lambda j: (0, j))],
        out_specs=pl.BlockSpec((m, tn), lambda j: (0, j)),
        out_shape=jax.ShapeDtypeStruct((m, n), F32),
        compiler_params=_cparams(("arbitrary",)),
        name="modulation",
    )(cc, w, b.reshape(1, n))


def _in_proj_kernel(x_ref, g_ref, sh_ref, sc_ref, w_ref, o_ref, h_ref):
    @pl.when(pl.program_id(2) == 0)
    def _():
        h_ref[...] = _mod_rmsnorm(x_ref[0], g_ref[...], sh_ref[0], sc_ref[0]).astype(BF16)

    o_ref[0] = jnp.dot(h_ref[...], w_ref[...], preferred_element_type=F32).astype(o_ref.dtype)


def _in_proj(x, g, sh, sc, w, tm, tn):
    b, s, d = x.shape
    n = w.shape[1]
    return pl.pallas_call(
        _in_proj_kernel,
        grid=(b, s // tm, n // tn),
        in_specs=[pl.BlockSpec((1, tm, d), lambda bi, i, j: (bi, i, 0)),
                  pl.BlockSpec((1, d), lambda bi, i, j: (0, 0)),
                  pl.BlockSpec((1, 1, d), lambda bi, i, j: (bi, 0, 0)),
                  pl.BlockSpec((1, 1, d), lambda bi, i, j: (bi, 0, 0)),
                  pl.BlockSpec((d, tn), lambda bi, i, j: (0, j))],
        out_specs=pl.BlockSpec((1, tm, tn), lambda bi, i, j: (bi, i, j)),
        out_shape=jax.ShapeDtypeStruct((b, s, n), BF16),
        scratch_shapes=[pltpu.VMEM((tm, d), BF16)],
        compiler_params=_cparams(("parallel", "parallel", "arbitrary")),
        name="in_proj",
    )(x, g.reshape(1, d), sh, sc, w)


ATT_GROUPS = (
    (((0, 8), (56, 8)), ((0, 16), (48, 16))),
    (((8, 16),), ((0, 32),)),
    (((24, 16),), ((16, 32),)),
    (((40, 16),), ((32, 32),)),
)
ATT_QG = 16
ATT_KG = 32
LANES = 2 * NA_HD


def _attn_latent_kernel(q_ref, k_ref, v_ref, kc_ref, vc_ref, qg_ref, kg_ref, tab_ref, o_ref, kn_ref, o_scr):
    j = pl.program_id(2)
    nj = pl.num_programs(2)
    wr = jnp.clip(QROWS * j - (WROWS - QROWS) // 2, 0, N_ROWS - WROWS)
    start = pl.multiple_of(wr * GRID_W, 256)
    ty = jnp.where(j == 0, 0, jnp.where(j == nj - 1, 2, 1))
    nwin = WROWS * GRID_W
    head0 = lax.broadcasted_iota(jnp.int32, (1, LANES), 1) < NA_HD

    def headnorm(x):
        sq = x * x
        s0 = jnp.sum(jnp.where(head0, sq, 0.0), axis=-1, keepdims=True)
        s1 = jnp.sum(jnp.where(head0, 0.0, sq), axis=-1, keepdims=True)
        return x * lax.rsqrt(jnp.where(head0, s0, s1) * (1.0 / NA_HD) + EPS)

    qn = headnorm(q_ref[0].astype(F32)) * qg_ref[...] * (NA_HD ** -0.5)
    kn_ref[...] = (headnorm(k_ref[0, pl.ds(start, nwin), :].astype(F32)) * kg_ref[...]).astype(BF16)
    kcn = (headnorm(kc_ref[0].astype(F32)) * kg_ref[...]).astype(BF16)
    vc2 = vc_ref[0]
    dn = (((1,), (1,)), ((), ()))
    qh = [jnp.where(head0, qn, 0.0), jnp.where(head0, 0.0, qn)]
    s_ctx = [lax.dot_general(q.astype(BF16), kcn, dn, preferred_element_type=F32) for q in qh]
    for g, (qchunks, kchunks) in enumerate(ATT_GROUPS):
        ct = 0 if g == 0 else 1
        qrows = [(qr * GRID_W + c, w) for qr in range(QROWS) for (c, w) in qchunks]
        krows = [(kr * GRID_W + c, w) for kr in range(WROWS) for (c, w) in kchunks]
        kg = jnp.concatenate([kn_ref[a:a + w, :] for a, w in krows], axis=0)
        vg = jnp.concatenate([v_ref[0, pl.ds(pl.multiple_of(start + a, 16), w), :] for a, w in krows], axis=0)
        o_heads = []
        for hh in range(2):
            qg = jnp.concatenate([qh[hh][a:a + w] for a, w in qrows], axis=0).astype(BF16)
            sc = jnp.concatenate([s_ctx[hh][a:a + w] for a, w in qrows], axis=0)
            s = lax.dot_general(qg, kg, dn, preferred_element_type=F32) + tab_ref[hh, ct * 3 + ty]
            m = jnp.maximum(jnp.max(s, axis=-1, keepdims=True), jnp.max(sc, axis=-1, keepdims=True))
            p = jnp.exp(s - m)
            pc = jnp.exp(sc - m)
            den = jnp.sum(p, axis=-1, keepdims=True) + jnp.sum(pc, axis=-1, keepdims=True)
            o = (jnp.dot(p.astype(BF16), vg, preferred_element_type=F32)
                 + jnp.dot(pc.astype(BF16), vc2, preferred_element_type=F32))
            o_heads.append(o / den)
        og = jnp.where(head0, o_heads[0], o_heads[1])
        r = 0
        for a, w in qrows:
            o_scr[a:a + w, :] = og[r:r + w]
            r += w
    o_ref[0] = o_scr[...].astype(o_ref.dtype)


def _attn_latent(px, pcx, kc_cb, vc_cb, qg, kg, table):
    b, s, _ = px.shape
    c = pcx.shape[1]
    tq = QROWS * GRID_W
    nj = s // tq
    nwin = WROWS * GRID_W
    nhp = NA_HEADS // 2
    q_cb, k_cb, v_cb = OFF_Q // LANES, OFF_K // LANES, OFF_V // LANES
    gain = lambda t: jnp.tile(t.reshape(1, NA_HD), (1, 2))
    return pl.pallas_call(
        _attn_latent_kernel,
        grid=(b, nhp, nj),
        in_specs=[pl.BlockSpec((1, tq, LANES), lambda bi, hp, j: (bi, j, q_cb + hp)),
                  pl.BlockSpec((1, s, LANES), lambda bi, hp, j: (bi, 0, k_cb + hp)),
                  pl.BlockSpec((1, s, LANES), lambda bi, hp, j: (bi, 0, v_cb + hp)),
                  pl.BlockSpec((1, c, LANES), lambda bi, hp, j: (bi, 0, kc_cb + hp)),
                  pl.BlockSpec((1, c, LANES), lambda bi, hp, j: (bi, 0, vc_cb + hp)),
                  pl.BlockSpec((1, LANES), lambda bi, hp, j: (0, 0)),
                  pl.BlockSpec((1, LANES), lambda bi, hp, j: (0, 0)),
                  pl.BlockSpec((2, 6, QROWS * ATT_QG, WROWS * ATT_KG), lambda bi, hp, j: (hp, 0, 0, 0))],
        out_specs=pl.BlockSpec((1, tq, LANES), lambda bi, hp, j: (bi, j, hp)),
        out_shape=jax.ShapeDtypeStruct((b, s, D_NA), BF16),
        scratch_shapes=[pltpu.VMEM((nwin, LANES), BF16), pltpu.VMEM((tq, LANES), F32)],
        compiler_params=_cparams(("parallel", "parallel", "arbitrary")),
        name="attn_latent",
    )(px, px, px, pcx, pcx, gain(qg), gain(kg), table)


def _attn_ctx_kernel(q_ref, k_ref, v_ref, qg_ref, kg_ref, o_ref):
    qn = (_rms(q_ref[0, 0].astype(F32)) * qg_ref[...] * (NA_HD ** -0.5)).astype(BF16)
    kn = (_rms(k_ref[0, 0].astype(F32)) * kg_ref[...]).astype(BF16)
    s = lax.dot_general(qn, kn, (((1,), (1,)), ((), ())), preferred_element_type=F32)
    m = jnp.max(s, axis=-1, keepdims=True)
    p = jnp.exp(s - m)
    den = jnp.sum(p, axis=-1, keepdims=True)
    o = jnp.dot(p.astype(BF16), v_ref[0, 0], preferred_element_type=F32)
    o_ref[0, 0] = (o / den).astype(o_ref.dtype)


def _attn_ctx(q, k, v, qg, kg):
    b, h, c, hd = q.shape
    spec = pl.BlockSpec((1, 1, c, hd), lambda bi, hi: (bi, hi, 0, 0))
    gspec = pl.BlockSpec((1, hd), lambda bi, hi: (0, 0))
    return pl.pallas_call(
        _attn_ctx_kernel,
        grid=(b, h),
        in_specs=[spec, spec, spec, gspec, gspec],
        out_specs=spec,
        out_shape=jax.ShapeDtypeStruct((b, h, c, hd), BF16),
        compiler_params=_cparams(("parallel", "parallel")),
        name="attn_ctx",
    )(q, k, v, qg.reshape(1, hd), kg.reshape(1, hd))


N_DR = 2 * NA_KH - 1
N_DC = 2 * NA_KW - 1
DR_SLOTS = 24
DC_PAD = 32


@functools.lru_cache(maxsize=None)
def _bias_constants():
    onehots, oks = [], []
    for qchunks, kchunks in ATT_GROUPS[:2]:
        qc = np.concatenate([np.arange(c, c + w) for c, w in qchunks])[:, None]
        kc = np.tile(np.concatenate([np.arange(c, c + w) for c, w in kchunks]), LANES // ATT_KG)[None, :]
        cs = np.clip(qc - NA_KW // 2, 0, GRID_W - NA_KW)
        ok = (kc >= cs) & (kc < cs + NA_KW)
        dc = np.where(ok, kc - qc + NA_KW - 1, DC_PAD - 1)
        onehots.append((np.arange(DC_PAD)[:, None, None] == dc[None]) & ok[None])
        oks.append(ok)
    onehot = np.stack(onehots, axis=1)
    colneg = np.where(np.stack(oks), 0.0, NEG).reshape(1, -1)
    rowneg = np.full((DR_SLOTS, 1), NEG)
    rowneg[1:N_DR + 1] = 0.0
    f = np.float32
    return onehot.reshape(DC_PAD, -1).astype(f), colneg.astype(f), rowneg.astype(f)


def _bias_cols_kernel(rpb_ref, oh_ref, colneg_ref, rowneg_ref, o_ref):
    sel = jnp.dot(rpb_ref[0], oh_ref[...], preferred_element_type=F32, precision=HIGHEST)
    o_ref[0] = sel + colneg_ref[...] + rowneg_ref[...]


def _bias_table_kernel(m_ref, o_ref):
    lane = lax.broadcasted_iota(jnp.int32, (ATT_QG, LANES), 1)
    nj = N_ROWS // QROWS
    per = LANES // ATT_KG
    for ct in range(2):
        for ty, j in enumerate((0, 1, nj - 1)):
            wr = min(max(QROWS * j - (WROWS - QROWS) // 2, 0), N_ROWS - WROWS)
            for qr in range(QROWS):
                r = QROWS * j + qr
                rs = min(max(r - NA_KH // 2, 0), N_ROWS - NA_KH)
                slots = [kr - r + NA_KH if rs <= kr < rs + NA_KH else 0 for kr in range(wr, wr + WROWS)]
                for kq in range(WROWS // per):
                    e = slots[per * kq:per * (kq + 1)]
                    blk = m_ref[0, e[per - 1], ct]
                    for i in range(per - 2, -1, -1):
                        blk = jnp.where(lane < ATT_KG * (i + 1), m_ref[0, e[i], ct], blk)
                    o_ref[0, ct * 3 + ty, qr * ATT_QG:(qr + 1) * ATT_QG, kq * LANES:(kq + 1) * LANES] = blk


def _bias_table(rpb):
    h = rpb.shape[0]
    onehot, colneg, rowneg = _bias_constants()
    rp = jnp.zeros((h, DR_SLOTS, DC_PAD), F32).at[:, 1:N_DR + 1, :N_DC].set(rpb)
    ncol = 2 * ATT_QG * LANES
    cols = pl.pallas_call(
        _bias_cols_kernel,
        grid=(h,),
        in_specs=[pl.BlockSpec((1, DR_SLOTS, DC_PAD), lambda hi: (hi, 0, 0)),
                  pl.BlockSpec((DC_PAD, ncol), lambda hi: (0, 0)),
                  pl.BlockSpec((1, ncol), lambda hi: (0, 0)),
                  pl.BlockSpec((DR_SLOTS, 1), lambda hi: (0, 0))],
        out_specs=pl.BlockSpec((1, DR_SLOTS, ncol), lambda hi: (hi, 0, 0)),
        out_shape=jax.ShapeDtypeStruct((h, DR_SLOTS, ncol), F32),
        compiler_params=_cparams(("parallel",)),
        name="bias_cols",
    )(rp, jnp.asarray(onehot), jnp.asarray(colneg), jnp.asarray(rowneg))
    tqg, nkg = QROWS * ATT_QG, WROWS * ATT_KG
    return pl.pallas_call(
        _bias_table_kernel,
        grid=(h,),
        in_specs=[pl.BlockSpec((1, DR_SLOTS, 2, ATT_QG, LANES), lambda hi: (hi, 0, 0, 0, 0))],
        out_specs=pl.BlockSpec((1, 6, tqg, nkg), lambda hi: (hi, 0, 0, 0)),
        out_shape=jax.ShapeDtypeStruct((h, 6, tqg, nkg), F32),
        compiler_params=_cparams(("parallel",)),
        name="bias_table",
    )(cols.reshape(h, DR_SLOTS, 2, ATT_QG, LANES))


def _col_specs(off, width, tm, s):
    return [pl.BlockSpec((1, tm, CB), functools.partial(lambda bi, i, cb: (bi, i, cb), cb=off // CB + t))
            for t in range(width // CB)]


def _halo_specs(off, width, tm, s):
    r = tm // HALO
    last = s // HALO - 1
    specs = []
    for t in range(width // CB):
        cb = off // CB + t
        specs.append(pl.BlockSpec((1, HALO, CB),
                                  functools.partial(lambda bi, i, cb: (bi, jnp.maximum(i * r - 1, 0), cb), cb=cb)))
        specs.append(pl.BlockSpec((1, HALO, CB),
                                  functools.partial(lambda bi, i, cb: (bi, jnp.minimum((i + 1) * r, last), cb), cb=cb)))
    return specs


def _extend(main, prev, nxt, first, last):
    prev = jnp.where(first, 0.0, prev.astype(F32))
    nxt = jnp.where(last, 0.0, nxt.astype(F32))
    return jnp.concatenate([prev, main.astype(F32), nxt], axis=0)


def _dwconv_ext(ext, w, tm):
    k = w.shape[0]
    p = (k - 1) // 2
    n = ext.shape[0]
    acc = None
    for t in range(k):
        d = t - p
        sh = ext if d == 0 else pltpu.roll(ext, (-d) % n, 0)
        term = sh * w[t:t + 1, :]
        acc = term if acc is None else acc + term
    return acc[HALO:HALO + tm]


def _hy_pre_kernel(*refs, tm):
    mains = refs[0:3]
    halos = refs[3:9]
    w_ref = refs[9]
    x0_ref, u_ref = refs[10], refs[11]
    i = pl.program_id(1)
    first = i == 0
    last = i == pl.num_programs(1) - 1
    outs = []
    for t in range(3):
        ext = _extend(mains[t][0], halos[2 * t][0], halos[2 * t + 1][0], first, last)
        outs.append(_dwconv_ext(ext, w_ref[:, t * CB:(t + 1) * CB], tm))
    x0_ref[0] = outs[0].astype(x0_ref.dtype)
    u_ref[0] = outs[1] * outs[2]


def _hy_pre(px, conv_h, tm):
    b, s, _ = px.shape
    in_specs = (_col_specs(OFF_HY, 3 * D_HY, tm, s) + _halo_specs(OFF_HY, 3 * D_HY, tm, s)
                + [pl.BlockSpec((SC_K, 3 * D_HY), lambda bi, i: (0, 0))])
    ospec = pl.BlockSpec((1, tm, D_HY), lambda bi, i: (bi, i, 0))
    return pl.pallas_call(
        functools.partial(_hy_pre_kernel, tm=tm),
        grid=(b, s // tm),
        in_specs=in_specs,
        out_specs=[ospec, ospec],
        out_shape=[jax.ShapeDtypeStruct((b, s, D_HY), BF16), jax.ShapeDtypeStruct((b, s, D_HY), F32)],
        compiler_params=_cparams(("parallel", "parallel")),
        name="hyena_pre",
    )(*([px] * 9), conv_h)


@functools.lru_cache(maxsize=None)
def _filter_features(length):
    t = np.linspace(0.0, 1.0, length)[:, None]
    bands = (HY_EMB - 1) // 2
    w = 2.0 * math.pi * np.arange(length)[:, None] / length
    f = np.linspace(1e-4, bands - 1, bands)[None, :]
    z = np.concatenate([t, np.cos(f * w), -np.sin(f * w)], axis=-1)
    pos = np.concatenate([np.arange(length), [0], np.arange(length - 1, 0, -1)])
    zp = np.zeros((2 * length, HY_FH), np.float32)
    zp[:, :HY_EMB] = z[pos]
    zp[:, HY_EMB] = 1.0
    zp[length, HY_EMB] = 0.0
    deltas = np.abs(np.linspace(math.log(HY_TARGET) / HY_SLOW, math.log(HY_TARGET) / HY_FAST, D_HY))
    return zp, np.tile(deltas, 2)[None, :].astype(np.float32)


def _filter_kernel(z_ref, w1_ref, b1_ref, w2_ref, b2_ref, w3_ref, b3_ref, w4_ref, fr_ref, dl_ref, o_ref):
    z = z_ref[...]
    fr = fr_ref[...]
    dot = functools.partial(jnp.dot, preferred_element_type=F32, precision=HIGHEST)
    h = jnp.sin(fr * (dot(z, w1_ref[...]) + b1_ref[...]))
    h = jnp.sin(fr * (dot(h, w2_ref[...]) + b2_ref[...]))
    h = jnp.sin(fr * (dot(h, w3_ref[...]) + b3_ref[...]))
    h = dot(h, w4_ref[...])
    window = jnp.exp(-z[:, 0:1] * dl_ref[...]) + HY_SHIFT
    o_ref[...] = h * window * z[:, HY_EMB:HY_EMB + 1]


def _hyena_conv_kernel(length, w1, b1, w2, b2, w3, b3, w4, freq):
    zp, deltas = _filter_features(length)
    w1p = jnp.zeros((HY_FH, HY_FH), F32).at[:HY_EMB].set(w1)
    tl = min(length, 512)
    half = length // tl
    full = lambda shape: pl.BlockSpec(shape, lambda i: (0, 0))
    return pl.pallas_call(
        _filter_kernel,
        grid=(2 * half,),
        in_specs=[pl.BlockSpec((tl, HY_FH), lambda i: (i, 0)),
                  full((HY_FH, HY_FH)), full((1, HY_FH)), full((HY_FH, HY_FH)), full((1, HY_FH)),
                  full((HY_FH, HY_FH)), full((1, HY_FH)),
                  pl.BlockSpec((HY_FH, D_HY), lambda i: (0, i // half)), full((1, HY_FH)),
                  pl.BlockSpec((1, D_HY), lambda i: (0, i // half))],
        out_specs=pl.BlockSpec((tl, D_HY), lambda i: (i, 0)),
        out_shape=jax.ShapeDtypeStruct((2 * length, D_HY), F32),
        compiler_params=_cparams(("parallel",)),
        name="hyena_filter",
    )(jnp.asarray(zp), w1p, b1.reshape(1, -1), w2, b2.reshape(1, -1), w3, b3.reshape(1, -1), w4,
      freq.reshape(1, -1), jnp.asarray(deltas))


FFT_N1 = 64
FFT_N2 = 128
FFT_N = FFT_N1 * FFT_N2


@functools.lru_cache(maxsize=None)
def _fft_constants():
    n1, n2, n = FFT_N1, FFT_N2, FFT_N
    k1 = np.arange(n1)[:, None]
    t1 = np.arange(n1)[None, :]
    a1 = 2.0 * math.pi * k1 * t1 / n1
    c1, s1 = np.cos(a1), np.sin(a1)
    h = n1 // 2
    w_data = np.zeros((n1, 2, 2 * h))
    w_data[:, 0, :h], w_data[:, 0, h:] = c1[:, :h], s1[:, :h]
    w_data[:, 1, :h], w_data[:, 1, h:] = -s1[:, :h], c1[:, :h]
    w_kern = np.stack([c1, -s1], axis=1)
    w_inv = np.zeros((2, h, n1, 2))
    w_inv[0, :, :, 0], w_inv[0, :, :, 1] = c1[:, :h].T, -s1[:, :h].T
    w_inv[1, :, :, 0], w_inv[1, :, :, 1] = s1[:, :h].T, c1[:, :h].T
    w_inv /= n
    t2 = np.arange(n2)[None, :]
    at = 2.0 * math.pi * k1 * t2 / n
    twr, twi = np.cos(at), -np.sin(at)
    k2 = np.arange(n2)[:, None]
    a2 = 2.0 * math.pi * k2 * t2 / n2
    c2, s2 = np.cos(a2), np.sin(a2)
    w2f = np.block([[c2, s2], [-s2, c2]])
    w2i = np.block([[c2, -s2], [s2, c2]])
    f = np.float32
    return dict(w_data=w_data.reshape(2 * n1, 2 * h).astype(f), w_kern=w_kern.reshape(2 * n1, n1).astype(f),
                w_inv=w_inv.reshape(2 * h, 2 * n1).astype(f), twr=twr[:, :, None].astype(f),
                twi=twi[:, :, None].astype(f), w2f=w2f.astype(f), w2i=w2i.astype(f))


DFT_SPLITS = 1
DFT_STORE = BF16 if DFT_SPLITS == 1 else F32
MID_KB = 4


def _split(x):
    x = x.astype(F32)
    hi = x.astype(BF16)
    if DFT_SPLITS == 1:
        return [hi]
    return [hi, (x - hi.astype(F32)).astype(BF16)]


def _dft_dot(a, b):
    acc = None
    for i, ai in enumerate(_split(a)):
        for j, bj in enumerate(_split(b)):
            if i + j < DFT_SPLITS:
                t = jnp.dot(ai, bj, preferred_element_type=F32)
                acc = t if acc is None else acc + t
    return acc


DFT_TB = 16
VLANES = 128


def _lmm_kernel(w_ref, z_ref, o_ref, zin_ref, out_ref):
    k, tb, ch = z_ref.shape[1:]
    m = o_ref.shape[1]
    nl = ch // VLANES
    for c in range(nl):
        zin_ref[c] = z_ref[0, :, :, c * VLANES:(c + 1) * VLANES].astype(F32).reshape(k * tb, VLANES)
    w = w_ref[...]
    for t in range(tb):
        zt = jnp.concatenate([zin_ref[c, pl.ds(t, k, stride=tb), :] for c in range(nl)], axis=-1)
        res = _dft_dot(w, zt)
        for c in range(nl):
            out_ref[c, pl.ds(t, m, stride=tb), :] = res[:, c * VLANES:(c + 1) * VLANES]
    for c in range(nl):
        o_ref[0, :, :, c * VLANES:(c + 1) * VLANES] = out_ref[c].reshape(m, tb, VLANES).astype(o_ref.dtype)


def _left_matmul(w, z, out_dtype):
    g, k, nt, ch = z.shape
    m = w.shape[0]
    tb = DFT_TB
    return pl.pallas_call(
        _lmm_kernel,
        grid=(g, nt // tb),
        in_specs=[pl.BlockSpec((m, k), lambda gi, j: (0, 0)),
                  pl.BlockSpec((1, k, tb, ch), lambda gi, j: (gi, 0, j, 0))],
        out_specs=pl.BlockSpec((1, m, tb, ch), lambda gi, j: (gi, 0, j, 0)),
        out_shape=jax.ShapeDtypeStruct((g, m, nt, ch), out_dtype),
        scratch_shapes=[pltpu.VMEM((ch // VLANES, k * tb, VLANES), F32),
                        pltpu.VMEM((ch // VLANES, m * tb, VLANES), F32)],
        compiler_params=_cparams(("parallel", "parallel")),
        name="dft_stage1",
    )(w, z)


def _twiddle_fwd(a, twr, twi):
    n2 = FFT_N2
    ar, ai = a[:n2].astype(F32), a[n2:].astype(F32)
    return ar * twr - ai * twi, ar * twi + ai * twr


def _stage2(w_ref, re, im):
    n2 = FFT_N2
    out = _dft_dot(w_ref[:, :n2], re) + _dft_dot(w_ref[:, n2:], im)
    return out[:n2], out[n2:]


def _kspec_kernel(a_ref, twr_ref, twi_ref, w2f_ref, o_ref):
    for t in range(MID_KB):
        br, bi = _twiddle_fwd(a_ref[0, t], twr_ref[t], twi_ref[t])
        xr, xi = _stage2(w2f_ref, br, bi)
        o_ref[t, :FFT_N2, :] = xr
        o_ref[t, FFT_N2:, :] = xi


def _fft_mid_kernel(a_ref, kf_ref, twr_ref, twi_ref, w2f_ref, w2i_ref, o_ref):
    n2 = FFT_N2
    for t in range(MID_KB):
        twr, twi = twr_ref[t], twi_ref[t]
        br, bi = _twiddle_fwd(a_ref[0, t], twr, twi)
        xr, xi = _stage2(w2f_ref, br, bi)
        kr, ki = kf_ref[t, :n2, :], kf_ref[t, n2:, :]
        yr = xr * kr - xi * ki
        yi = xr * ki + xi * kr
        cr, ci = _stage2(w2i_ref, yr, yi)
        o_ref[0, t, :n2, :] = (cr * twr + ci * twi).astype(o_ref.dtype)
        o_ref[0, t, n2:, :] = (ci * twr - cr * twi).astype(o_ref.dtype)


def _long_conv_latent(u, kern):
    b, l, ch = u.shape
    n1, n2 = FFT_N1, FFT_N2
    cst = _fft_constants()
    cols = n2 * ch
    kb = MID_KB
    tw_spec = pl.BlockSpec((kb, n2, 1), lambda k1, p: (k1, 0, 0))
    w2_spec = pl.BlockSpec((2 * n2, 2 * n2), lambda k1, p: (0, 0))
    ak = _left_matmul(jnp.asarray(cst["w_kern"]), kern.reshape(1, n1, n2, ch), DFT_STORE)
    kf = pl.pallas_call(
        _kspec_kernel,
        grid=(n1 // kb, 1),
        in_specs=[pl.BlockSpec((1, kb, 2 * n2, ch), lambda k1, p: (0, k1, 0, 0)), tw_spec, tw_spec, w2_spec],
        out_specs=pl.BlockSpec((kb, 2 * n2, ch), lambda k1, p: (k1, 0, 0)),
        out_shape=jax.ShapeDtypeStruct((n1, 2 * n2, ch), F32),
        compiler_params=_cparams(("parallel", "arbitrary")),
        name="dft_kernel_spectrum",
    )(ak.reshape(1, n1, 2 * n2, ch), jnp.asarray(cst["twr"]), jnp.asarray(cst["twi"]), jnp.asarray(cst["w2f"]))
    npair = b // 2
    a = _left_matmul(jnp.asarray(cst["w_data"]), u.reshape(npair, n1, n2, ch), DFT_STORE)
    d = pl.pallas_call(
        _fft_mid_kernel,
        grid=(n1 // kb, npair),
        in_specs=[pl.BlockSpec((1, kb, 2 * n2, ch), lambda k1, p: (p, k1, 0, 0)),
                  pl.BlockSpec((kb, 2 * n2, ch), lambda k1, p: (k1, 0, 0)),
                  tw_spec, tw_spec, w2_spec, w2_spec],
        out_specs=pl.BlockSpec((1, kb, 2 * n2, ch), lambda k1, p: (p, k1, 0, 0)),
        out_shape=jax.ShapeDtypeStruct((npair, n1, 2 * n2, ch), DFT_STORE),
        compiler_params=_cparams(("parallel", "arbitrary")),
        name="dft_mid",
    )(a.reshape(npair, n1, 2 * n2, ch), kf, jnp.asarray(cst["twr"]), jnp.asarray(cst["twi"]),
      jnp.asarray(cst["w2f"]), jnp.asarray(cst["w2i"]))
    y = _left_matmul(jnp.asarray(cst["w_inv"]), d.reshape(npair, 2 * n1, n2, ch), F32)
    return y.reshape(b, l, ch)


@functools.lru_cache(maxsize=None)
def _dft_constants(n):
    k = np.arange(n)[:, None]
    t = np.arange(n)[None, :]
    ang = 2.0 * math.pi * k * t / n
    return np.cos(ang).astype(np.float32), (-np.sin(ang)).astype(np.float32)


def _conv_ctx_kernel(u_ref, kern_ref, fr_ref, fi_ref, o_ref):
    l = u_ref.shape[1]
    n = 2 * l
    dot = functools.partial(jnp.dot, preferred_element_type=F32, precision=HIGHEST)
    fr, fi = fr_ref[...], fi_ref[...]
    kern = kern_ref[...]
    kr, ki = dot(fr, kern), dot(fi, kern)
    u = u_ref[0]
    ur, ui = dot(fr[:, :l], u), dot(fi[:, :l], u)
    yr = ur * kr - ui * ki
    yi = ur * ki + ui * kr
    o_ref[0] = (dot(fr[:l, :], yr) + dot(fi[:l, :], yi)) * (1.0 / n)


def _long_conv_ctx(u, kern):
    b, l, ch = u.shape
    n = 2 * l
    fr, fi = _dft_constants(n)
    return pl.pallas_call(
        _conv_ctx_kernel,
        grid=(b,),
        in_specs=[pl.BlockSpec((1, l, ch), lambda bi: (bi, 0, 0)),
                  pl.BlockSpec((n, ch), lambda bi: (0, 0)),
                  pl.BlockSpec((n, n), lambda bi: (0, 0)),
                  pl.BlockSpec((n, n), lambda bi: (0, 0))],
        out_specs=pl.BlockSpec((1, l, ch), lambda bi: (bi, 0, 0)),
        out_shape=jax.ShapeDtypeStruct((b, l, ch), F32),
        compiler_params=_cparams(("parallel",)),
        name="long_conv_ctx",
    )(u, kern, jnp.asarray(fr), jnp.asarray(fi))


def _mix_kernel(*refs, tm):
    it = iter(refs)
    x_ref, g1_ref = next(it), next(it)
    pa = [next(it) for _ in range(3)]
    pa_h = [next(it) for _ in range(6)]
    attn_ref, x0_ref, u_ref, yh_ref = next(it), next(it), next(it), next(it)
    pcf = [next(it) for _ in range(2)]
    pcf_h = [next(it) for _ in range(4)]
    pg = [next(it) for _ in range(8)]
    (conva_ref, wa_ref, wna_ref, skip_ref, wh_ref, convd_ref, convdb_ref, lng_ref, lnb_ref, wd_ref,
     wo_ref, o_ref) = [next(it) for _ in range(12)]

    i = pl.program_id(1)
    first = i == 0
    last = i == pl.num_programs(1) - 1
    dot = functools.partial(jnp.dot, preferred_element_type=F32)

    ext_x = _extend(pa[0][0], pa_h[0][0], pa_h[1][0], first, last)
    ext_c = _extend(pa[2][0], pa_h[4][0], pa_h[5][0], first, last)
    za = pa[1][0].astype(F32) * _dwconv_ext(ext_c * ext_x, conva_ref[...], tm)
    y_a = dot(za.astype(BF16), wa_ref[...])
    y_na = dot(attn_ref[0], wna_ref[...])
    u = u_ref[0]
    zh = x0_ref[0].astype(F32) * (yh_ref[0] + u * skip_ref[...])
    y_hy = dot(zh.astype(BF16), wh_ref[...])
    ext_a = _extend(pcf[0][0], pcf_h[0][0], pcf_h[1][0], first, last)
    ext_g = _extend(pcf[1][0], pcf_h[2][0], pcf_h[3][0], first, last)
    uc = _dwconv_ext(ext_a * _sigmoid(ext_g), convd_ref[...], tm) + convdb_ref[...]
    mu = jnp.mean(uc, axis=-1, keepdims=True)
    dv = uc - mu
    var = jnp.mean(dv * dv, axis=-1, keepdims=True)
    zc = _silu(dv * lax.rsqrt(var + EPS) * lng_ref[...] + lnb_ref[...])
    y_cf = dot(zc.astype(BF16), wd_ref[...])

    halves = []
    for hf in range(2):
        sl = slice(hf * CB, (hf + 1) * CB)
        halves.append(_sigmoid(pg[0 + hf][0]) * y_a[:, sl].astype(BF16)
                      + _sigmoid(pg[2 + hf][0]) * y_na[:, sl].astype(BF16)
                      + _sigmoid(pg[4 + hf][0]) * y_hy[:, sl].astype(BF16)
                      + _sigmoid(pg[6 + hf][0]) * y_cf[:, sl].astype(BF16))
    merged = jnp.concatenate(halves, axis=-1)
    o_ref[0] = x_ref[0] + g1_ref[0] * dot(merged, wo_ref[...])


def _mix(x, g1, px, attn, x0, u, yh, wts, tm):
    b, s, d = x.shape
    row = lambda width: pl.BlockSpec((1, tm, width), lambda bi, i: (bi, i, 0))
    full2 = lambda a: pl.BlockSpec(a.shape, lambda bi, i: (0, 0))
    in_specs = ([row(d), pl.BlockSpec((1, 1, d), lambda bi, i: (bi, 0, 0))]
                + _col_specs(OFF_A, 3 * D_A, tm, s) + _halo_specs(OFF_A, 3 * D_A, tm, s)
                + [row(D_NA), row(D_HY), row(D_HY), row(D_HY)]
                + _col_specs(OFF_CF, 2 * D_CF, tm, s) + _halo_specs(OFF_CF, 2 * D_CF, tm, s)
                + _col_specs(OFF_G, N_BRANCH * D_MODEL, tm, s)
                + [full2(w) for w in wts])
    n_px = 3 + 6 + 2 + 4 + 8
    args = [x, g1] + [px] * 9 + [attn, x0, u, yh] + [px] * 14 + list(wts)
    assert len(args) == len(in_specs) and n_px == 23
    return pl.pallas_call(
        functools.partial(_mix_kernel, tm=tm),
        grid=(b, s // tm),
        in_specs=in_specs,
        out_specs=row(d),
        out_shape=jax.ShapeDtypeStruct((b, s, d), F32),
        compiler_params=_cparams(("parallel", "parallel")),
        name="mixer_out",
    )(*args)


FFN_CHUNK = 256


def _ffn_kernel(x_ref, g_ref, sh_ref, sc_ref, gate_ref, wa_ref, wu_ref, wo_ref, o_ref):
    x = x_ref[0]
    h = _mod_rmsnorm(x, g_ref[...], sh_ref[0], sc_ref[0]).astype(BF16)
    acc = None
    for c in range(wo_ref.shape[0] // FFN_CHUNK):
        cols = slice(c * FFN_CHUNK, (c + 1) * FFN_CHUNK)
        a = jnp.dot(h, wa_ref[:, cols], preferred_element_type=F32)
        up = jnp.dot(h, wu_ref[:, cols], preferred_element_type=F32)
        act = (_silu(a) * up).astype(BF16)
        part = jnp.dot(act, wo_ref[cols, :], preferred_element_type=F32)
        acc = part if acc is None else acc + part
    o_ref[0] = x + gate_ref[0] * acc


def _ffn(x, g, sh, sc, gate, w_in, w_out, tm):
    b, s, d = x.shape
    dff = w_out.shape[0]
    vec = pl.BlockSpec((1, 1, d), lambda bi, i: (bi, 0, 0))
    once = pl.Buffered(1)
    return pl.pallas_call(
        _ffn_kernel,
        grid=(b, s // tm),
        in_specs=[pl.BlockSpec((1, tm, d), lambda bi, i: (bi, i, 0)),
                  pl.BlockSpec((1, d), lambda bi, i: (0, 0)),
                  vec, vec, vec,
                  pl.BlockSpec((d, dff), lambda bi, i: (0, 0), pipeline_mode=once),
                  pl.BlockSpec((d, dff), lambda bi, i: (0, 1), pipeline_mode=once),
                  pl.BlockSpec((dff, d), lambda bi, i: (0, 0), pipeline_mode=once)],
        out_specs=pl.BlockSpec((1, tm, d), lambda bi, i: (bi, i, 0)),
        out_shape=jax.ShapeDtypeStruct((b, s, d), F32),
        compiler_params=_cparams(("parallel", "parallel")),
        name="ffn",
    )(x, g.reshape(1, d), sh, sc, gate, w_in, w_in, w_out)


def _heads(t):
    b, s, _ = t.shape
    return t.reshape(b, s, NA_HEADS, NA_HD).transpose(0, 2, 1, 3)


def _unheads(t):
    b, h, s, hd = t.shape
    return t.transpose(0, 2, 1, 3).reshape(b, s, h * hd)


def kernel(x, c, ctx, c_ctx, w_mod, b_mod, g_norm1, g_norm2, w_in, conv_a, w_a_out, q_gain, k_gain, rpb, w_na_out, conv_h, filt_w1, filt_b1, filt_w2, filt_b2, filt_w3, filt_b3, filt_w4, filt_freq, hy_skip, w_h_out, conv_d, conv_d_b, ln_g, ln_b, w_d_out, w_o, w_ffn_in, w_ffn_out):
    depth = w_mod.shape[0]
    b, s, d = x.shape
    n_ctx = ctx.shape[1]
    cc = jnp.zeros((8, d), F32).at[:b].set(c).at[b].set(c_ctx)

    for l in range(depth):
        last = l == depth - 1
        mods = _modulation(cc, w_mod[l], b_mod[l])
        lat = [mods[:b, t * d:(t + 1) * d].reshape(b, 1, d) for t in range(6)]
        cxm = [jnp.broadcast_to(mods[b:b + 1, t * d:(t + 1) * d].reshape(1, 1, d), (b, 1, d)) for t in range(6)]
        sh1, sc1, g1, sh2, sc2, g2 = lat
        csh1, csc1, cg1, csh2, csc2, cg2 = cxm

        w_in_b = w_in[l].astype(BF16)
        mix_w = (conv_a[l], w_a_out[l].astype(BF16), w_na_out[l].astype(BF16), hy_skip[l].reshape(1, -1),
                 w_h_out[l].astype(BF16), conv_d[l], conv_d_b[l].reshape(1, -1), ln_g[l].reshape(1, -1),
                 ln_b[l].reshape(1, -1), w_d_out[l].astype(BF16), w_o[l].astype(BF16))
        w_ffn_in_b = w_ffn_in[l].astype(BF16)
        w_ffn_out_b = w_ffn_out[l].astype(BF16)
        filt_args = (filt_w1[l], filt_b1[l], filt_w2[l], filt_b2[l], filt_w3[l], filt_b3[l], filt_w4[l], filt_freq[l])

        px = _in_proj(x, g_norm1[l], sh1, sc1, w_in_b, 1024, 2432)
        if last:
            pcx = _in_proj(ctx, g_norm1[l], csh1, csc1, w_in_b[:, OFF_K:OFF_HY], n_ctx, 2 * D_NA)
            kc_cb, vc_cb = 0, D_NA // LANES
        else:
            pc = pcx = _in_proj(ctx, g_norm1[l], csh1, csc1, w_in_b, n_ctx, 2432)
            kc_cb, vc_cb = OFF_K // LANES, OFF_V // LANES

        table = _bias_table(rpb[l])
        attn_x = _attn_latent(px, pcx, kc_cb, vc_cb, q_gain[l], k_gain[l], table)

        x0, u = _hy_pre(px, conv_h[l], 512)
        kern = _hyena_conv_kernel(s, *filt_args)
        yh = _long_conv_latent(u, kern)
        x_new = _mix(x, g1, px, attn_x, x0, u, yh, mix_w, 256)

        if not last:
            attn_c = _unheads(_attn_ctx(_heads(pc[..., OFF_Q:OFF_K]), _heads(pc[..., OFF_K:OFF_V]),
                                        _heads(pc[..., OFF_V:OFF_HY]), q_gain[l], k_gain[l]))
            x0c, uc = _hy_pre(pc, conv_h[l], n_ctx)
            kern_c = _hyena_conv_kernel(n_ctx, *filt_args)
            yhc = _long_conv_ctx(uc, kern_c)
            ctx = _mix(ctx, cg1, pc, attn_c, x0c, uc, yhc, mix_w, n_ctx)
            ctx = _ffn(ctx, g_norm2[l], csh2, csc2, cg2, w_ffn_in_b, w_ffn_out_b, n_ctx)

        x = _ffn(x_new, g_norm2[l], sh2, sc2, g2, w_ffn_in_b, w_ffn_out_b, 512)
    return x
```

```python
import functools
import math

import numpy as np
import jax
import jax.numpy as jnp
from jax import lax
from jax.experimental import pallas as pl
from jax.experimental.pallas import tpu as pltpu

F32 = jnp.float32
BF16 = jnp.bfloat16
HIGHEST = lax.Precision.HIGHEST

D_MODEL = 1024
GRID_W = 64
N_BRANCH = 4
D_A = D_MODEL // 2
SC_K = 3
D_NA = D_MODEL // 2
NA_HEADS = 8
NA_HD = D_NA // NA_HEADS
NA_KH = 8
NA_KW = 16
D_HY = D_MODEL // 2
HY_EMB = 33
HY_FH = 64
HY_SHIFT = 0.05
HY_FAST = 0.3
HY_SLOW = 1.5
HY_TARGET = 1e-2
D_CF = D_MODEL // 2
CF_K = 31
D_FF = ((8 * D_MODEL + 3 * 256 - 1) // (3 * 256)) * 256
EPS = 1e-6
OFF_A = 0
OFF_Q = OFF_A + 3 * D_A
OFF_K = OFF_Q + D_NA
OFF_V = OFF_K + D_NA
OFF_HY = OFF_V + D_NA
OFF_CF = OFF_HY + 3 * D_HY
OFF_G = OFF_CF + 2 * D_CF
N_IN = OFF_G + N_BRANCH * D_MODEL

CB = 512
HALO = 16
NEG = -1e30
VMEM_LIMIT = 56 * 1024 * 1024

QROWS = 8
WROWS = 16
N_ROWS = 64


def _cparams(sem):
    return pltpu.CompilerParams(dimension_semantics=sem, vmem_limit_bytes=VMEM_LIMIT)


def _sigmoid(x):
    return 1.0 / (1.0 + jnp.exp(-x))


def _silu(x):
    return x * _sigmoid(x)


def _rms(x):
    return x * lax.rsqrt(jnp.mean(x * x, axis=-1, keepdims=True) + EPS)


def _mod_rmsnorm(x, g, shift, scale):
    return (_rms(x) * g) * (1.0 + scale) + shift


def _mod_kernel(c_ref, w_ref, b_ref, o_ref):
    s = _silu(c_ref[...])
    o_ref[...] = jnp.dot(s, w_ref[...], preferred_element_type=F32, precision=HIGHEST) + b_ref[...]


def _modulation(cc, w, b):
    m, d = cc.shape
    n = w.shape[1]
    tn = 1536
    return pl.pallas_call(
        _mod_kernel,
        grid=(n // tn,),
        in_specs=[pl.BlockSpec((m, d), lambda j: (0, 0)),
                  pl.BlockSpec((d, tn), lambda j: (0, j)),
                  pl.BlockSpec((1, tn), lambda j: (0, j))],
        out_specs=pl.BlockSpec((m, tn), lambda j: (0, j)),
        out_shape=jax.ShapeDtypeStruct((m, n), F32),
        compiler_params=_cparams(("arbitrary",)),
        name="modulation",
    )(cc, w, b.reshape(1, n))


def _in_proj_kernel(x_ref, g_ref, sh_ref, sc_ref, w_ref, o_ref, h_ref):
    @pl.when(pl.program_id(2) == 0)
    def _():
        h_ref[...] = _mod_rmsnorm(x_ref[0], g_ref[...], sh_ref[0], sc_ref[0]).astype(BF16)

    o_ref[0] = jnp.dot(h_ref[...], w_ref[...], preferred_element_type=F32).astype(o_ref.dtype)


def _in_proj(x, g, sh, sc, w, tm, tn):
    b, s, d = x.shape
    n = w.shape[1]
    return pl.pallas_call(
        _in_proj_kernel,
        grid=(b, s // tm, n // tn),
        in_specs=[pl.BlockSpec((1, tm, d), lambda bi, i, j: (bi, i, 0)),
                  pl.BlockSpec((1, d), lambda bi, i, j: (0, 0)),
                  pl.BlockSpec((1, 1, d), lambda bi, i, j: (bi, 0, 0)),
                  pl.BlockSpec((1, 1, d), lambda bi, i, j: (bi, 0, 0)),
                  pl.BlockSpec((d, tn), lambda bi, i, j: (0, j))],
        out_specs=pl.BlockSpec((1, tm, tn), lambda bi, i, j: (bi, i, j)),
        out_shape=jax.ShapeDtypeStruct((b, s, n), BF16),
        scratch_shapes=[pltpu.VMEM((tm, d), BF16)],
        compiler_params=_cparams(("parallel", "parallel", "arbitrary")),
        name="in_proj",
    )(x, g.reshape(1, d), sh, sc, w)


ATT_GROUPS = (
    (((0, 8), (56, 8)), ((0, 16), (48, 16))),
    (((8, 16),), ((0, 32),)),
    (((24, 16),), ((16, 32),)),
    (((40, 16),), ((32, 32),)),
)
ATT_QG = 16
ATT_KG = 32
LANES = 2 * NA_HD


def _attn_latent_kernel(q_ref, k_ref, v_ref, kc_ref, vc_ref, qg_ref, kg_ref, tab_ref, o_ref, kn_ref, o_scr):
    j = pl.program_id(2)
    nj = pl.num_programs(2)
    wr = jnp.clip(QROWS * j - (WROWS - QROWS) // 2, 0, N_ROWS - WROWS)
    start = pl.multiple_of(wr * GRID_W, 256)
    ty = jnp.where(j == 0, 0, jnp.where(j == nj - 1, 2, 1))
    nwin = WROWS * GRID_W
    head0 = lax.broadcasted_iota(jnp.int32, (1, LANES), 1) < NA_HD

    def headnorm(x):
        sq = x * x
        s0 = jnp.sum(jnp.where(head0, sq, 0.0), axis=-1, keepdims=True)
        s1 = jnp.sum(jnp.where(head0, 0.0, sq), axis=-1, keepdims=True)
        return x * lax.rsqrt(jnp.where(head0, s0, s1) * (1.0 / NA_HD) + EPS)

    qn = headnorm(q_ref[0].astype(F32)) * qg_ref[...] * (NA_HD ** -0.5)
    kn_ref[...] = (headnorm(k_ref[0, pl.ds(start, nwin), :].astype(F32)) * kg_ref[...]).astype(BF16)
    kcn = (headnorm(kc_ref[0].astype(F32)) * kg_ref[...]).astype(BF16)
    vc2 = vc_ref[0]
    dn = (((1,), (1,)), ((), ()))
    qh = [jnp.where(head0, qn, 0.0), jnp.where(head0, 0.0, qn)]
    s_ctx = [lax.dot_general(q.astype(BF16), kcn, dn, preferred_element_type=F32) for q in qh]
    for g, (qchunks, kchunks) in enumerate(ATT_GROUPS):
        ct = 0 if g == 0 else 1
        qrows = [(qr * GRID_W + c, w) for qr in range(QROWS) for (c, w) in qchunks]
        krows = [(kr * GRID_W + c, w) for kr in range(WROWS) for (c, w) in kchunks]
        kg = jnp.concatenate([kn_ref[a:a + w, :] for a, w in krows], axis=0)
        vg = jnp.concatenate([v_ref[0, pl.ds(pl.multiple_of(start + a, 16), w), :] for a, w in krows], axis=0)
        qg = jnp.concatenate([qh[hh][a:a + w] for hh in range(2) for a, w in qrows], axis=0).astype(BF16)
        sc = jnp.concatenate([s_ctx[hh][a:a + w] for hh in range(2) for a, w in qrows], axis=0)
        tab = jnp.concatenate([tab_ref[0, ct * 3 + ty], tab_ref[1, ct * 3 + ty]], axis=0)
        s = lax.dot_general(qg, kg, dn, preferred_element_type=F32) + tab
        m = jnp.maximum(jnp.max(s, axis=-1, keepdims=True), jnp.max(sc, axis=-1, keepdims=True))
        p = jnp.exp(s - m)
        pc = jnp.exp(sc - m)
        den = jnp.sum(p, axis=-1, keepdims=True) + jnp.sum(pc, axis=-1, keepdims=True)
        o = (jnp.dot(p.astype(BF16), vg, preferred_element_type=F32)
             + jnp.dot(pc.astype(BF16), vc2, preferred_element_type=F32)) / den
        nq = QROWS * ATT_QG
        og = jnp.where(head0, o[:nq], o[nq:])
        r = 0
        for a, w in qrows:
            o_scr[a:a + w, :] = og[r:r + w]
            r += w
    o_ref[0] = o_scr[...].astype(o_ref.dtype)


def _attn_latent(px, pcx, kc_cb, vc_cb, qg, kg, table):
    b, s, _ = px.shape
    c = pcx.shape[1]
    tq = QROWS * GRID_W
    nj = s // tq
    nwin = WROWS * GRID_W
    nhp = NA_HEADS // 2
    q_cb, k_cb, v_cb = OFF_Q // LANES, OFF_K // LANES, OFF_V // LANES
    gain = lambda t: jnp.tile(t.reshape(1, NA_HD), (1, 2))
    return pl.pallas_call(
        _attn_latent_kernel,
        grid=(b, nhp, nj),
        in_specs=[pl.BlockSpec((1, tq, LANES), lambda bi, hp, j: (bi, j, q_cb + hp)),
                  pl.BlockSpec((1, s, LANES), lambda bi, hp, j: (bi, 0, k_cb + hp)),
                  pl.BlockSpec((1, s, LANES), lambda bi, hp, j: (bi, 0, v_cb + hp)),
                  pl.BlockSpec((1, c, LANES), lambda bi, hp, j: (bi, 0, kc_cb + hp)),
                  pl.BlockSpec((1, c, LANES), lambda bi, hp, j: (bi, 0, vc_cb + hp)),
                  pl.BlockSpec((1, LANES), lambda bi, hp, j: (0, 0)),
                  pl.BlockSpec((1, LANES), lambda bi, hp, j: (0, 0)),
                  pl.BlockSpec((2, 6, QROWS * ATT_QG, WROWS * ATT_KG), lambda bi, hp, j: (hp, 0, 0, 0))],
        out_specs=pl.BlockSpec((1, tq, LANES), lambda bi, hp, j: (bi, j, hp)),
        out_shape=jax.ShapeDtypeStruct((b, s, D_NA), BF16),
        scratch_shapes=[pltpu.VMEM((nwin, LANES), BF16), pltpu.VMEM((tq, LANES), F32)],
        compiler_params=_cparams(("parallel", "parallel", "arbitrary")),
        name="attn_latent",
    )(px, px, px, pcx, pcx, gain(qg), gain(kg), table)


def _attn_ctx_kernel(q_ref, k_ref, v_ref, qg_ref, kg_ref, o_ref):
    qn = (_rms(q_ref[0, 0].astype(F32)) * qg_ref[...] * (NA_HD ** -0.5)).astype(BF16)
    kn = (_rms(k_ref[0, 0].astype(F32)) * kg_ref[...]).astype(BF16)
    s = lax.dot_general(qn, kn, (((1,), (1,)), ((), ())), preferred_element_type=F32)
    m = jnp.max(s, axis=-1, keepdims=True)
    p = jnp.exp(s - m)
    den = jnp.sum(p, axis=-1, keepdims=True)
    o = jnp.dot(p.astype(BF16), v_ref[0, 0], preferred_element_type=F32)
    o_ref[0, 0] = (o / den).astype(o_ref.dtype)


def _attn_ctx(q, k, v, qg, kg):
    b, h, c, hd = q.shape
    spec = pl.BlockSpec((1, 1, c, hd), lambda bi, hi: (bi, hi, 0, 0))
    gspec = pl.BlockSpec((1, hd), lambda bi, hi: (0, 0))
    return pl.pallas_call(
        _attn_ctx_kernel,
        grid=(b, h),
        in_specs=[spec, spec, spec, gspec, gspec],
        out_specs=spec,
        out_shape=jax.ShapeDtypeStruct((b, h, c, hd), BF16),
        compiler_params=_cparams(("parallel", "parallel")),
        name="attn_ctx",
    )(q, k, v, qg.reshape(1, hd), kg.reshape(1, hd))


N_DR = 2 * NA_KH - 1
N_DC = 2 * NA_KW - 1
DR_SLOTS = 24
DC_PAD = 32


@functools.lru_cache(maxsize=None)
def _bias_constants():
    onehots, oks = [], []
    for qchunks, kchunks in ATT_GROUPS[:2]:
        qc = np.concatenate([np.arange(c, c + w) for c, w in qchunks])[:, None]
        kc = np.tile(np.concatenate([np.arange(c, c + w) for c, w in kchunks]), LANES // ATT_KG)[None, :]
        cs = np.clip(qc - NA_KW // 2, 0, GRID_W - NA_KW)
        ok = (kc >= cs) & (kc < cs + NA_KW)
        dc = np.where(ok, kc - qc + NA_KW - 1, DC_PAD - 1)
        onehots.append((np.arange(DC_PAD)[:, None, None] == dc[None]) & ok[None])
        oks.append(ok)
    onehot = np.stack(onehots, axis=1)
    colneg = np.where(np.stack(oks), 0.0, NEG).reshape(1, -1)
    rowneg = np.full((DR_SLOTS, 1), NEG)
    rowneg[1:N_DR + 1] = 0.0
    f = np.float32
    return onehot.reshape(DC_PAD, -1).astype(f), colneg.astype(f), rowneg.astype(f)


def _bias_cols_kernel(rpb_ref, oh_ref, colneg_ref, rowneg_ref, o_ref):
    sel = jnp.dot(rpb_ref[0], oh_ref[...], preferred_element_type=F32, precision=HIGHEST)
    o_ref[0] = sel + colneg_ref[...] + rowneg_ref[...]


def _bias_table_kernel(m_ref, o_ref):
    lane = lax.broadcasted_iota(jnp.int32, (ATT_QG, LANES), 1)
    nj = N_ROWS // QROWS
    per = LANES // ATT_KG
    for ct in range(2):
        for ty, j in enumerate((0, 1, nj - 1)):
            wr = min(max(QROWS * j - (WROWS - QROWS) // 2, 0), N_ROWS - WROWS)
            for qr in range(QROWS):
                r = QROWS * j + qr
                rs = min(max(r - NA_KH // 2, 0), N_ROWS - NA_KH)
                slots = [kr - r + NA_KH if rs <= kr < rs + NA_KH else 0 for kr in range(wr, wr + WROWS)]
                for kq in range(WROWS // per):
                    e = slots[per * kq:per * (kq + 1)]
                    blk = m_ref[0, e[per - 1], ct]
                    for i in range(per - 2, -1, -1):
                        blk = jnp.where(lane < ATT_KG * (i + 1), m_ref[0, e[i], ct], blk)
                    o_ref[0, ct * 3 + ty, qr * ATT_QG:(qr + 1) * ATT_QG, kq * LANES:(kq + 1) * LANES] = blk


def _bias_table(rpb):
    h = rpb.shape[0]
    onehot, colneg, rowneg = _bias_constants()
    rp = jnp.zeros((h, DR_SLOTS, DC_PAD), F32).at[:, 1:N_DR + 1, :N_DC].set(rpb)
    ncol = 2 * ATT_QG * LANES
    cols = pl.pallas_call(
        _bias_cols_kernel,
        grid=(h,),
        in_specs=[pl.BlockSpec((1, DR_SLOTS, DC_PAD), lambda hi: (hi, 0, 0)),
                  pl.BlockSpec((DC_PAD, ncol), lambda hi: (0, 0)),
                  pl.BlockSpec((1, ncol), lambda hi: (0, 0)),
                  pl.BlockSpec((DR_SLOTS, 1), lambda hi: (0, 0))],
        out_specs=pl.BlockSpec((1, DR_SLOTS, ncol), lambda hi: (hi, 0, 0)),
        out_shape=jax.ShapeDtypeStruct((h, DR_SLOTS, ncol), F32),
        compiler_params=_cparams(("parallel",)),
        name="bias_cols",
    )(rp, jnp.asarray(onehot), jnp.asarray(colneg), jnp.asarray(rowneg))
    tqg, nkg = QROWS * ATT_QG, WROWS * ATT_KG
    return pl.pallas_call(
        _bias_table_kernel,
        grid=(h,),
        in_specs=[pl.BlockSpec((1, DR_SLOTS, 2, ATT_QG, LANES), lambda hi: (hi, 0, 0, 0, 0))],
        out_specs=pl.BlockSpec((1, 6, tqg, nkg), lambda hi: (hi, 0, 0, 0)),
        out_shape=jax.ShapeDtypeStruct((h, 6, tqg, nkg), F32),
        compiler_params=_cparams(("parallel",)),
        name="bias_table",
    )(cols.reshape(h, DR_SLOTS, 2, ATT_QG, LANES))


def _col_specs(off, width, tm, s):
    return [pl.BlockSpec((1, tm, CB), functools.partial(lambda bi, i, cb: (bi, i, cb), cb=off // CB + t))
            for t in range(width // CB)]


def _halo_specs(off, width, tm, s):
    r = tm // HALO
    last = s // HALO - 1
    specs = []
    for t in range(width // CB):
        cb = off // CB + t
        specs.append(pl.BlockSpec((1, HALO, CB),
                                  functools.partial(lambda bi, i, cb: (bi, jnp.maximum(i * r - 1, 0), cb), cb=cb)))
        specs.append(pl.BlockSpec((1, HALO, CB),
                                  functools.partial(lambda bi, i, cb: (bi, jnp.minimum((i + 1) * r, last), cb), cb=cb)))
    return specs


def _extend(main, prev, nxt, first, last):
    prev = jnp.where(first, 0.0, prev.astype(F32))
    nxt = jnp.where(last, 0.0, nxt.astype(F32))
    return jnp.concatenate([prev, main.astype(F32), nxt], axis=0)


def _dwconv_ext(ext, w, tm):
    k = w.shape[0]
    p = (k - 1) // 2
    n = ext.shape[0]
    acc = None
    for t in range(k):
        d = t - p
        sh = ext if d == 0 else pltpu.roll(ext, (-d) % n, 0)
        term = sh * w[t:t + 1, :]
        acc = term if acc is None else acc + term
    return acc[HALO:HALO + tm]


def _hy_pre_kernel(*refs, tm):
    mains = refs[0:3]
    halos = refs[3:9]
    w_ref = refs[9]
    x0_ref, u_ref = refs[10], refs[11]
    i = pl.program_id(1)
    first = i == 0
    last = i == pl.num_programs(1) - 1
    outs = []
    for t in range(3):
        ext = _extend(mains[t][0], halos[2 * t][0], halos[2 * t + 1][0], first, last)
        outs.append(_dwconv_ext(ext, w_ref[:, t * CB:(t + 1) * CB], tm))
    x0_ref[0] = outs[0].astype(x0_ref.dtype)
    u_ref[0] = outs[1] * outs[2]


def _hy_pre(px, conv_h, tm):
    b, s, _ = px.shape
    in_specs = (_col_specs(OFF_HY, 3 * D_HY, tm, s) + _halo_specs(OFF_HY, 3 * D_HY, tm, s)
                + [pl.BlockSpec((SC_K, 3 * D_HY), lambda bi, i: (0, 0))])
    ospec = pl.BlockSpec((1, tm, D_HY), lambda bi, i: (bi, i, 0))
    return pl.pallas_call(
        functools.partial(_hy_pre_kernel, tm=tm),
        grid=(b, s // tm),
        in_specs=in_specs,
        out_specs=[ospec, ospec],
        out_shape=[jax.ShapeDtypeStruct((b, s, D_HY), BF16), jax.ShapeDtypeStruct((b, s, D_HY), F32)],
        compiler_params=_cparams(("parallel", "parallel")),
        name="hyena_pre",
    )(*([px] * 9), conv_h)


@functools.lru_cache(maxsize=None)
def _filter_features(length):
    t = np.linspace(0.0, 1.0, length)[:, None]
    bands = (HY_EMB - 1) // 2
    w = 2.0 * math.pi * np.arange(length)[:, None] / length
    f = np.linspace(1e-4, bands - 1, bands)[None, :]
    z = np.concatenate([t, np.cos(f * w), -np.sin(f * w)], axis=-1)
    pos = np.concatenate([np.arange(length), [0], np.arange(length - 1, 0, -1)])
    zp = np.zeros((2 * length, HY_FH), np.float32)
    zp[:, :HY_EMB] = z[pos]
    zp[:, HY_EMB] = 1.0
    zp[length, HY_EMB] = 0.0
    tl = _filter_tile(length)
    zp2 = zp.reshape(length // tl, 2, tl, HY_FH).transpose(0, 2, 1, 3).reshape(length, 2 * HY_FH)
    deltas = np.abs(np.linspace(math.log(HY_TARGET) / HY_SLOW, math.log(HY_TARGET) / HY_FAST, D_HY))
    return zp2, np.tile(deltas, 2)[None, :].astype(np.float32)


def _filter_tile(length):
    return min(length, 1024) // 2


def _filter_kernel(z_ref, w1_ref, b1_ref, w2_ref, b2_ref, w3_ref, b3_ref, w4a_ref, w4b_ref, fr_ref, dl_ref, o_ref):
    tl = z_ref.shape[0]
    z = z_ref[...]
    fr = fr_ref[...]
    dot = functools.partial(jnp.dot, preferred_element_type=F32, precision=HIGHEST)
    h = jnp.sin(fr * (dot(z, w1_ref[...]) + b1_ref[...]))
    h = jnp.sin(fr * (dot(h, w2_ref[...]) + b2_ref[...]))
    h = jnp.sin(fr * (dot(h, w3_ref[...]) + b3_ref[...]))
    dl = dl_ref[...]
    for part, w4_ref in enumerate((w4a_ref, w4b_ref)):
        c0 = part * HY_FH
        window = jnp.exp(-z[:, c0:c0 + 1] * dl) + HY_SHIFT
        o_ref[part * tl:(part + 1) * tl, :] = dot(h, w4_ref[...]) * window * z[:, c0 + HY_EMB:c0 + HY_EMB + 1]


def _hyena_conv_kernel(length, w1, b1, w2, b2, w3, b3, w4, freq):
    zp2, deltas = _filter_features(length)
    fh = HY_FH
    zero = jnp.zeros((fh, fh), F32)
    diag2 = lambda m: jnp.concatenate([jnp.concatenate([m, zero], 1), jnp.concatenate([zero, m], 1)], 0)
    twice = lambda v: jnp.tile(v.reshape(1, -1), (1, 2))
    w1p = jnp.zeros((fh, fh), F32).at[:HY_EMB].set(w1)
    w4z = jnp.zeros_like(w4)
    w4a = jnp.concatenate([w4, w4z], axis=0)
    w4b = jnp.concatenate([w4z, w4], axis=0)
    tl = _filter_tile(length)
    half = length // (2 * tl)
    full = lambda shape: pl.BlockSpec(shape, lambda i: (0, 0))
    w4_spec = pl.BlockSpec((2 * fh, D_HY), lambda i: (0, i // half))
    return pl.pallas_call(
        _filter_kernel,
        grid=(2 * half,),
        in_specs=[pl.BlockSpec((tl, 2 * fh), lambda i: (i, 0)),
                  full((2 * fh, 2 * fh)), full((1, 2 * fh)), full((2 * fh, 2 * fh)), full((1, 2 * fh)),
                  full((2 * fh, 2 * fh)), full((1, 2 * fh)),
                  w4_spec, w4_spec, full((1, 2 * fh)),
                  pl.BlockSpec((1, D_HY), lambda i: (0, i // half))],
        out_specs=pl.BlockSpec((2 * tl, D_HY), lambda i: (i, 0)),
        out_shape=jax.ShapeDtypeStruct((2 * length, D_HY), F32),
        compiler_params=_cparams(("parallel",)),
        name="hyena_filter",
    )(jnp.asarray(zp2), diag2(w1p), twice(b1), diag2(w2), twice(b2), diag2(w3), twice(b3), w4a, w4b,
      twice(freq), jnp.asarray(deltas))


FFT_N1 = 64
FFT_N2 = 128
FFT_N = FFT_N1 * FFT_N2


@functools.lru_cache(maxsize=None)
def _fft_constants():
    n1, n2, n = FFT_N1, FFT_N2, FFT_N
    k1 = np.arange(n1)[:, None]
    t1 = np.arange(n1)[None, :]
    a1 = 2.0 * math.pi * k1 * t1 / n1
    c1, s1 = np.cos(a1), np.sin(a1)
    h = n1 // 2
    w_data = np.zeros((n1, 2, 2 * h))
    w_data[:, 0, :h], w_data[:, 0, h:] = c1[:, :h], s1[:, :h]
    w_data[:, 1, :h], w_data[:, 1, h:] = -s1[:, :h], c1[:, :h]
    w_kern = np.stack([c1, -s1], axis=1)
    w_inv = np.zeros((2, h, n1, 2))
    w_inv[0, :, :, 0], w_inv[0, :, :, 1] = c1[:, :h].T, -s1[:, :h].T
    w_inv[1, :, :, 0], w_inv[1, :, :, 1] = s1[:, :h].T, c1[:, :h].T
    w_inv /= n
    t2 = np.arange(n2)[None, :]
    at = 2.0 * math.pi * k1 * t2 / n
    twr, twi = np.cos(at), -np.sin(at)
    k2 = np.arange(n2)[:, None]
    a2 = 2.0 * math.pi * k2 * t2 / n2
    c2, s2 = np.cos(a2), np.sin(a2)
    w2f = np.block([[c2, s2], [-s2, c2]])
    w2i = np.block([[c2, -s2], [s2, c2]])
    f = np.float32
    return dict(w_data=w_data.reshape(2 * n1, 2 * h).astype(f), w_kern=w_kern.reshape(2 * n1, n1).astype(f),
                w_inv=w_inv.reshape(2 * h, 2 * n1).astype(f), twr=twr[:, :, None].astype(f),
                twi=twi[:, :, None].astype(f), w2f=w2f.astype(f), w2i=w2i.astype(f))


DFT_SPLITS = 1
DFT_STORE = BF16 if DFT_SPLITS == 1 else F32
MID_KB = 4


def _split(x):
    x = x.astype(F32)
    hi = x.astype(BF16)
    if DFT_SPLITS == 1:
        return [hi]
    return [hi, (x - hi.astype(F32)).astype(BF16)]


def _dft_dot(a, b):
    acc = None
    for i, ai in enumerate(_split(a)):
        for j, bj in enumerate(_split(b)):
            if i + j < DFT_SPLITS:
                t = jnp.dot(ai, bj, preferred_element_type=F32)
                acc = t if acc is None else acc + t
    return acc


DFT_TB = 16
DFT_PITCH = 24
VLANES = 128


def _lmm_kernel(w_ref, z_ref, o_ref, zin_ref, out_ref):
    k, tb, ch = z_ref.shape[1:]
    m = o_ref.shape[1]
    nl = ch // VLANES
    pitch = DFT_PITCH
    for c in range(nl):
        lanes = slice(c * VLANES, (c + 1) * VLANES)
        for r in range(k):
            zin_ref[c, r * pitch:r * pitch + tb, :] = z_ref[0, r, :, lanes].astype(F32)
    w = w_ref[...]
    for t in range(tb):
        zt = jnp.concatenate([zin_ref[c, pl.ds(t, k, stride=pitch), :] for c in range(nl)], axis=-1)
        res = _dft_dot(w, zt)
        for c in range(nl):
            out_ref[c, pl.ds(t, m, stride=pitch), :] = res[:, c * VLANES:(c + 1) * VLANES]
    for c in range(nl):
        lanes = slice(c * VLANES, (c + 1) * VLANES)
        for r in range(m):
            o_ref[0, r, :, lanes] = out_ref[c, r * pitch:r * pitch + tb, :].astype(o_ref.dtype)


def _left_matmul(w, z, out_dtype):
    g, k, nt, ch = z.shape
    m = w.shape[0]
    tb = DFT_TB
    return pl.pallas_call(
        _lmm_kernel,
        grid=(g, nt // tb),
        in_specs=[pl.BlockSpec((m, k), lambda gi, j: (0, 0)),
                  pl.BlockSpec((1, k, tb, ch), lambda gi, j: (gi, 0, j, 0))],
        out_specs=pl.BlockSpec((1, m, tb, ch), lambda gi, j: (gi, 0, j, 0)),
        out_shape=jax.ShapeDtypeStruct((g, m, nt, ch), out_dtype),
        scratch_shapes=[pltpu.VMEM((ch // VLANES, k * DFT_PITCH, VLANES), F32),
                        pltpu.VMEM((ch // VLANES, m * DFT_PITCH, VLANES), F32)],
        compiler_params=_cparams(("parallel", "parallel")),
        name="dft_stage1",
    )(w, z)


def _twiddle_fwd(a, twr, twi):
    n2 = FFT_N2
    ar, ai = a[:n2].astype(F32), a[n2:].astype(F32)
    return ar * twr - ai * twi, ar * twi + ai * twr


def _stage2(w_ref, re, im):
    n2 = FFT_N2
    out = _dft_dot(w_ref[:, :n2], re) + _dft_dot(w_ref[:, n2:], im)
    return out[:n2], out[n2:]


def _kspec_kernel(a_ref, twr_ref, twi_ref, w2f_ref, o_ref):
    for t in range(MID_KB):
        br, bi = _twiddle_fwd(a_ref[0, t], twr_ref[t], twi_ref[t])
        xr, xi = _stage2(w2f_ref, br, bi)
        o_ref[t, :FFT_N2, :] = xr
        o_ref[t, FFT_N2:, :] = xi


def _fft_mid_kernel(a_ref, kf_ref, twr_ref, twi_ref, w2f_ref, w2i_ref, o_ref):
    n2 = FFT_N2
    for t in range(MID_KB):
        twr, twi = twr_ref[t], twi_ref[t]
        br, bi = _twiddle_fwd(a_ref[0, t], twr, twi)
        xr, xi = _stage2(w2f_ref, br, bi)
        kr, ki = kf_ref[t, :n2, :], kf_ref[t, n2:, :]
        yr = xr * kr - xi * ki
        yi = xr * ki + xi * kr
        cr, ci = _stage2(w2i_ref, yr, yi)
        o_ref[0, t, :n2, :] = (cr * twr + ci * twi).astype(o_ref.dtype)
        o_ref[0, t, n2:, :] = (ci * twr - cr * twi).astype(o_ref.dtype)


def _long_conv_latent(u, kern):
    b, l, ch = u.shape
    n1, n2 = FFT_N1, FFT_N2
    cst = _fft_constants()
    cols = n2 * ch
    kb = MID_KB
    tw_spec = pl.BlockSpec((kb, n2, 1), lambda k1, p: (k1, 0, 0))
    w2_spec = pl.BlockSpec((2 * n2, 2 * n2), lambda k1, p: (0, 0))
    ak = _left_matmul(jnp.asarray(cst["w_kern"]), kern.reshape(1, n1, n2, ch), DFT_STORE)
    kf = pl.pallas_call(
        _kspec_kernel,
        grid=(n1 // kb, 1),
        in_specs=[pl.BlockSpec((1, kb, 2 * n2, ch), lambda k1, p: (0, k1, 0, 0)), tw_spec, tw_spec, w2_spec],
        out_specs=pl.BlockSpec((kb, 2 * n2, ch), lambda k1, p: (k1, 0, 0)),
        out_shape=jax.ShapeDtypeStruct((n1, 2 * n2, ch), F32),
        compiler_params=_cparams(("parallel", "arbitrary")),
        name="dft_kernel_spectrum",
    )(ak.reshape(1, n1, 2 * n2, ch), jnp.asarray(cst["twr"]), jnp.asarray(cst["twi"]), jnp.asarray(cst["w2f"]))
    npair = b // 2
    a = _left_matmul(jnp.asarray(cst["w_data"]), u.reshape(npair, n1, n2, ch), DFT_STORE)
    d = pl.pallas_call(
        _fft_mid_kernel,
        grid=(n1 // kb, npair),
        in_specs=[pl.BlockSpec((1, kb, 2 * n2, ch), lambda k1, p: (p, k1, 0, 0)),
                  pl.BlockSpec((kb, 2 * n2, ch), lambda k1, p: (k1, 0, 0)),
                  tw_spec, tw_spec, w2_spec, w2_spec],
        out_specs=pl.BlockSpec((1, kb, 2 * n2, ch), lambda k1, p: (p, k1, 0, 0)),
        out_shape=jax.ShapeDtypeStruct((npair, n1, 2 * n2, ch), DFT_STORE),
        compiler_params=_cparams(("parallel", "arbitrary")),
        name="dft_mid",
    )(a.reshape(npair, n1, 2 * n2, ch), kf, jnp.asarray(cst["twr"]), jnp.asarray(cst["twi"]),
      jnp.asarray(cst["w2f"]), jnp.asarray(cst["w2i"]))
    y = _left_matmul(jnp.asarray(cst["w_inv"]), d.reshape(npair, 2 * n1, n2, ch), F32)
    return y.reshape(b, l, ch)


@functools.lru_cache(maxsize=None)
def _dft_constants(n):
    k = np.arange(n)[:, None]
    t = np.arange(n)[None, :]
    ang = 2.0 * math.pi * k * t / n
    return np.cos(ang).astype(np.float32), (-np.sin(ang)).astype(np.float32)


def _conv_ctx_kernel(u_ref, kern_ref, fr_ref, fi_ref, o_ref):
    l = u_ref.shape[1]
    n = 2 * l
    dot = functools.partial(jnp.dot, preferred_element_type=F32, precision=HIGHEST)
    fr, fi = fr_ref[...], fi_ref[...]
    kern = kern_ref[...]
    kr, ki = dot(fr, kern), dot(fi, kern)
    u = u_ref[0]
    ur, ui = dot(fr[:, :l], u), dot(fi[:, :l], u)
    yr = ur * kr - ui * ki
    yi = ur * ki + ui * kr
    o_ref[0] = (dot(fr[:l, :], yr) + dot(fi[:l, :], yi)) * (1.0 / n)


def _long_conv_ctx(u, kern):
    b, l, ch = u.shape
    n = 2 * l
    fr, fi = _dft_constants(n)
    return pl.pallas_call(
        _conv_ctx_kernel,
        grid=(b,),
        in_specs=[pl.BlockSpec((1, l, ch), lambda bi: (bi, 0, 0)),
                  pl.BlockSpec((n, ch), lambda bi: (0, 0)),
                  pl.BlockSpec((n, n), lambda bi: (0, 0)),
                  pl.BlockSpec((n, n), lambda bi: (0, 0))],
        out_specs=pl.BlockSpec((1, l, ch), lambda bi: (bi, 0, 0)),
        out_shape=jax.ShapeDtypeStruct((b, l, ch), F32),
        compiler_params=_cparams(("parallel",)),
        name="long_conv_ctx",
    )(u, kern, jnp.asarray(fr), jnp.asarray(fi))


def _mix_kernel(*refs, tm):
    it = iter(refs)
    x_ref, g1_ref = next(it), next(it)
    pa = [next(it) for _ in range(3)]
    pa_h = [next(it) for _ in range(6)]
    attn_ref, x0_ref, u_ref, yh_ref = next(it), next(it), next(it), next(it)
    pcf = [next(it) for _ in range(2)]
    pcf_h = [next(it) for _ in range(4)]
    pg = [next(it) for _ in range(8)]
    (conva_ref, wa_ref, wna_ref, skip_ref, wh_ref, convd_ref, convdb_ref, lng_ref, lnb_ref, wd_ref,
     wo_ref, o_ref) = [next(it) for _ in range(12)]

    i = pl.program_id(1)
    first = i == 0
    last = i == pl.num_programs(1) - 1
    dot = functools.partial(jnp.dot, preferred_element_type=F32)

    ext_x = _extend(pa[0][0], pa_h[0][0], pa_h[1][0], first, last)
    ext_c = _extend(pa[2][0], pa_h[4][0], pa_h[5][0], first, last)
    za = pa[1][0].astype(F32) * _dwconv_ext(ext_c * ext_x, conva_ref[...], tm)
    y_a = dot(za.astype(BF16), wa_ref[...])
    y_na = dot(attn_ref[0], wna_ref[...])
    u = u_ref[0]
    zh = x0_ref[0].astype(F32) * (yh_ref[0] + u * skip_ref[...])
    y_hy = dot(zh.astype(BF16), wh_ref[...])
    ext_a = _extend(pcf[0][0], pcf_h[0][0], pcf_h[1][0], first, last)
    ext_g = _extend(pcf[1][0], pcf_h[2][0], pcf_h[3][0], first, last)
    uc = _dwconv_ext(ext_a * _sigmoid(ext_g), convd_ref[...], tm) + convdb_ref[...]
    mu = jnp.mean(uc, axis=-1, keepdims=True)
    dv = uc - mu
    var = jnp.mean(dv * dv, axis=-1, keepdims=True)
    zc = _silu(dv * lax.rsqrt(var + EPS) * lng_ref[...] + lnb_ref[...])
    y_cf = dot(zc.astype(BF16), wd_ref[...])

    halves = []
    for hf in range(2):
        sl = slice(hf * CB, (hf + 1) * CB)
        halves.append(_sigmoid(pg[0 + hf][0]) * y_a[:, sl].astype(BF16)
                      + _sigmoid(pg[2 + hf][0]) * y_na[:, sl].astype(BF16)
                      + _sigmoid(pg[4 + hf][0]) * y_hy[:, sl].astype(BF16)
                      + _sigmoid(pg[6 + hf][0]) * y_cf[:, sl].astype(BF16))
    merged = jnp.concatenate(halves, axis=-1)
    o_ref[0] = x_ref[0] + g1_ref[0] * dot(merged, wo_ref[...])


def _mix(x, g1, px, attn, x0, u, yh, wts, tm):
    b, s, d = x.shape
    row = lambda width: pl.BlockSpec((1, tm, width), lambda bi, i: (bi, i, 0))
    full2 = lambda a: pl.BlockSpec(a.shape, lambda bi, i: (0, 0))
    in_specs = ([row(d), pl.BlockSpec((1, 1, d), lambda bi, i: (bi, 0, 0))]
                + _col_specs(OFF_A, 3 * D_A, tm, s) + _halo_specs(OFF_A, 3 * D_A, tm, s)
                + [row(D_NA), row(D_HY), row(D_HY), row(D_HY)]
                + _col_specs(OFF_CF, 2 * D_CF, tm, s) + _halo_specs(OFF_CF, 2 * D_CF, tm, s)
                + _col_specs(OFF_G, N_BRANCH * D_MODEL, tm, s)
                + [full2(w) for w in wts])
    n_px = 3 + 6 + 2 + 4 + 8
    args = [x, g1] + [px] * 9 + [attn, x0, u, yh] + [px] * 14 + list(wts)
    assert len(args) == len(in_specs) and n_px == 23
    return pl.pallas_call(
        functools.partial(_mix_kernel, tm=tm),
        grid=(b, s // tm),
        in_specs=in_specs,
        out_specs=row(d),
        out_shape=jax.ShapeDtypeStruct((b, s, d), F32),
        compiler_params=_cparams(("parallel", "parallel")),
        name="mixer_out",
    )(*args)


FFN_CHUNK = 256


def _ffn_kernel(x_ref, g_ref, sh_ref, sc_ref, gate_ref, wa_ref, wu_ref, wo_ref, o_ref):
    x = x_ref[0]
    h = _mod_rmsnorm(x, g_ref[...], sh_ref[0], sc_ref[0]).astype(BF16)
    acc = None
    for c in range(wo_ref.shape[0] // FFN_CHUNK):
        cols = slice(c * FFN_CHUNK, (c + 1) * FFN_CHUNK)
        a = jnp.dot(h, wa_ref[:, cols], preferred_element_type=F32)
        up = jnp.dot(h, wu_ref[:, cols], preferred_element_type=F32)
        act = (_silu(a) * up).astype(BF16)
        part = jnp.dot(act, wo_ref[cols, :], preferred_element_type=F32)
        acc = part if acc is None else acc + part
    o_ref[0] = x + gate_ref[0] * acc


def _ffn(x, g, sh, sc, gate, w_in, w_out, tm):
    b, s, d = x.shape
    dff = w_out.shape[0]
    vec = pl.BlockSpec((1, 1, d), lambda bi, i: (bi, 0, 0))
    once = pl.Buffered(1)
    return pl.pallas_call(
        _ffn_kernel,
        grid=(b, s // tm),
        in_specs=[pl.BlockSpec((1, tm, d), lambda bi, i: (bi, i, 0)),
                  pl.BlockSpec((1, d), lambda bi, i: (0, 0)),
                  vec, vec, vec,
                  pl.BlockSpec((d, dff), lambda bi, i: (0, 0), pipeline_mode=once),
                  pl.BlockSpec((d, dff), lambda bi, i: (0, 1), pipeline_mode=once),
                  pl.BlockSpec((dff, d), lambda bi, i: (0, 0), pipeline_mode=once)],
        out_specs=pl.BlockSpec((1, tm, d), lambda bi, i: (bi, i, 0)),
        out_shape=jax.ShapeDtypeStruct((b, s, d), F32),
        compiler_params=_cparams(("parallel", "parallel")),
        name="ffn",
    )(x, g.reshape(1, d), sh, sc, gate, w_in, w_in, w_out)


def _heads(t):
    b, s, _ = t.shape
    return t.reshape(b, s, NA_HEADS, NA_HD).transpose(0, 2, 1, 3)


def _unheads(t):
    b, h, s, hd = t.shape
    return t.transpose(0, 2, 1, 3).reshape(b, s, h * hd)


def kernel(x, c, ctx, c_ctx, w_mod, b_mod, g_norm1, g_norm2, w_in, conv_a, w_a_out, q_gain, k_gain, rpb, w_na_out, conv_h, filt_w1, filt_b1, filt_w2, filt_b2, filt_w3, filt_b3, filt_w4, filt_freq, hy_skip, w_h_out, conv_d, conv_d_b, ln_g, ln_b, w_d_out, w_o, w_ffn_in, w_ffn_out):
    depth = w_mod.shape[0]
    b, s, d = x.shape
    n_ctx = ctx.shape[1]
    cc = jnp.zeros((8, d), F32).at[:b].set(c).at[b].set(c_ctx)

    for l in range(depth):
        last = l == depth - 1
        mods = _modulation(cc, w_mod[l], b_mod[l])
        lat = [mods[:b, t * d:(t + 1) * d].reshape(b, 1, d) for t in range(6)]
        cxm = [jnp.broadcast_to(mods[b:b + 1, t * d:(t + 1) * d].reshape(1, 1, d), (b, 1, d)) for t in range(6)]
        sh1, sc1, g1, sh2, sc2, g2 = lat
        csh1, csc1, cg1, csh2, csc2, cg2 = cxm

        w_in_b = w_in[l].astype(BF16)
        mix_w = (conv_a[l], w_a_out[l].astype(BF16), w_na_out[l].astype(BF16), hy_skip[l].reshape(1, -1),
                 w_h_out[l].astype(BF16), conv_d[l], conv_d_b[l].reshape(1, -1), ln_g[l].reshape(1, -1),
                 ln_b[l].reshape(1, -1), w_d_out[l].astype(BF16), w_o[l].astype(BF16))
        w_ffn_in_b = w_ffn_in[l].astype(BF16)
        w_ffn_out_b = w_ffn_out[l].astype(BF16)
        filt_args = (filt_w1[l], filt_b1[l], filt_w2[l], filt_b2[l], filt_w3[l], filt_b3[l], filt_w4[l], filt_freq[l])

        px = _in_proj(x, g_norm1[l], sh1, sc1, w_in_b, 1024, 2432)
        if last:
            pcx = _in_proj(ctx, g_norm1[l], csh1, csc1, w_in_b[:, OFF_K:OFF_HY], n_ctx, 2 * D_NA)
            kc_cb, vc_cb = 0, D_NA // LANES
        else:
            pc = pcx = _in_proj(ctx, g_norm1[l], csh1, csc1, w_in_b, n_ctx, 2432)
            kc_cb, vc_cb = OFF_K // LANES, OFF_V // LANES

        table = _bias_table(rpb[l])
        attn_x = _attn_latent(px, pcx, kc_cb, vc_cb, q_gain[l], k_gain[l], table)

        x0, u = _hy_pre(px, conv_h[l], 512)
        kern = _hyena_conv_kernel(s, *filt_args)
        yh = _long_conv_latent(u, kern)
        x_new = _mix(x, g1, px, attn_x, x0, u, yh, mix_w, 256)

        if not last:
            attn_c = _unheads(_attn_ctx(_heads(pc[..., OFF_Q:OFF_K]), _heads(pc[..., OFF_K:OFF_V]),
                                        _heads(pc[..., OFF_V:OFF_HY]), q_gain[l], k_gain[l]))
            x0c, uc = _hy_pre(pc, conv_h[l], n_ctx)
            kern_c = _hyena_conv_kernel(n_ctx, *filt_args)
            yhc = _long_conv_ctx(uc, kern_c)
            ctx = _mix(ctx, cg1, pc, attn_c, x0c, uc, yhc, mix_w, n_ctx)
            ctx = _ffn(ctx, g_norm2[l], csh2, csc2, cg2, w_ffn_in_b, w_ffn_out_b, n_ctx)

        x = _ffn(x_new, g_norm2[l], sh2, sc2, g2, w_ffn_in_b, w_ffn_out_b, 512)
    return x
```

```python
import functools
import math

import numpy as np
import jax
import jax.numpy as jnp
from jax import lax
from jax.experimental import pallas as pl
from jax.experimental.pallas import tpu as pltpu

F32 = jnp.float32
BF16 = jnp.bfloat16
HIGHEST = lax.Precision.HIGHEST

D_MODEL = 1024
GRID_W = 64
N_BRANCH = 4
D_A = D_MODEL // 2
SC_K = 3
D_NA = D_MODEL // 2
NA_HEADS = 8
NA_HD = D_NA // NA_HEADS
NA_KH = 8
NA_KW = 16
D_HY = D_MODEL // 2
HY_EMB = 33
HY_FH = 64
HY_SHIFT = 0.05
HY_FAST = 0.3
HY_SLOW = 1.5
HY_TARGET = 1e-2
D_CF = D_MODEL // 2
CF_K = 31
D_FF = ((8 * D_MODEL + 3 * 256 - 1) // (3 * 256)) * 256
EPS = 1e-6
OFF_A = 0
OFF_Q = OFF_A + 3 * D_A
OFF_K = OFF_Q + D_NA
OFF_V = OFF_K + D_NA
OFF_HY = OFF_V + D_NA
OFF_CF = OFF_HY + 3 * D_HY
OFF_G = OFF_CF + 2 * D_CF
N_IN = OFF_G + N_BRANCH * D_MODEL

CB = 512
HALO = 16
SUBLANES = 8
NEG = -1e30
VMEM_LIMIT = 56 * 1024 * 1024

QROWS = 8
WROWS = 16
N_ROWS = 64


def _cparams(sem):
    return pltpu.CompilerParams(dimension_semantics=sem, vmem_limit_bytes=VMEM_LIMIT)


def _sigmoid(x):
    return 1.0 / (1.0 + jnp.exp(-x))


def _silu(x):
    return x * _sigmoid(x)


def _rms(x):
    return x * lax.rsqrt(jnp.mean(x * x, axis=-1, keepdims=True) + EPS)


def _mod_rmsnorm(x, g, shift, scale):
    return (_rms(x) * g) * (1.0 + scale) + shift


def _mod_kernel(c_ref, w_ref, b_ref, o_ref):
    s = _silu(c_ref[...])
    o_ref[...] = jnp.dot(s, w_ref[...], preferred_element_type=F32, precision=HIGHEST) + b_ref[...]


def _modulation(cc, w, b):
    m, d = cc.shape
    n = w.shape[1]
    tn = 1536
    return pl.pallas_call(
        _mod_kernel,
        grid=(n // tn,),
        in_specs=[pl.BlockSpec((m, d), lambda j: (0, 0)),
                  pl.BlockSpec((d, tn), lambda j: (0, j)),
                  pl.BlockSpec((1, tn), lambda j: (0, j))],
        out_specs=pl.BlockSpec((m, tn), lambda j: (0, j)),
        out_shape=jax.ShapeDtypeStruct((m, n), F32),
        compiler_params=_cparams(("arbitrary",)),
        name="modulation",
    )(cc, w, b.reshape(1, n))


SIG_FROM = (OFF_CF + D_CF) // CB


def _in_proj_kernel(x_ref, g_ref, sh_ref, sc_ref, w_ref, o_ref, *, n_plain):
    h = _mod_rmsnorm(x_ref[0], g_ref[...], sh_ref[0], sc_ref[0]).astype(BF16)
    for c in range(w_ref.shape[1] // CB):
        cols = slice(c * CB, (c + 1) * CB)
        v = jnp.dot(h, w_ref[:, cols], preferred_element_type=F32)
        if c >= n_plain:
            v = _sigmoid(v)
        o_ref[0, :, cols] = v.astype(o_ref.dtype)


def _in_proj(x, g, sh, sc, w, tm, n_plain):
    b, s, d = x.shape
    n = w.shape[1]
    return pl.pallas_call(
        functools.partial(_in_proj_kernel, n_plain=n_plain),
        grid=(b, s // tm),
        in_specs=[pl.BlockSpec((1, tm, d), lambda bi, i: (bi, i, 0)),
                  pl.BlockSpec((1, d), lambda bi, i: (0, 0)),
                  pl.BlockSpec((1, 1, d), lambda bi, i: (bi, 0, 0)),
                  pl.BlockSpec((1, 1, d), lambda bi, i: (bi, 0, 0)),
                  pl.BlockSpec((d, n), lambda bi, i: (0, 0), pipeline_mode=pl.Buffered(1))],
        out_specs=pl.BlockSpec((1, tm, n), lambda bi, i: (bi, i, 0)),
        out_shape=jax.ShapeDtypeStruct((b, s, n), BF16),
        compiler_params=_cparams(("parallel", "parallel")),
        name="in_proj",
    )(x, g.reshape(1, d), sh, sc, w)


ATT_GROUPS = (
    (((0, 8), (56, 8)), ((0, 16), (48, 16))),
    (((8, 16),), ((0, 32),)),
    (((24, 16),), ((16, 32),)),
    (((40, 16),), ((32, 32),)),
)
ATT_QG = 16
ATT_KG = 32
LANES = 2 * NA_HD


def _attn_latent_kernel(q_ref, k_ref, v_ref, kc_ref, vc_ref, qg_ref, kg_ref, tab_ref, o_ref, kn_ref, o_scr):
    j = pl.program_id(2)
    nj = pl.num_programs(2)
    wr = jnp.clip(QROWS * j - (WROWS - QROWS) // 2, 0, N_ROWS - WROWS)
    start = pl.multiple_of(wr * GRID_W, 256)
    ty = jnp.where(j == 0, 0, jnp.where(j == nj - 1, 2, 1))
    nwin = WROWS * GRID_W
    head0 = lax.broadcasted_iota(jnp.int32, (1, LANES), 1) < NA_HD

    def headnorm(x):
        sq = x * x
        s0 = jnp.sum(jnp.where(head0, sq, 0.0), axis=-1, keepdims=True)
        s1 = jnp.sum(jnp.where(head0, 0.0, sq), axis=-1, keepdims=True)
        return x * lax.rsqrt(jnp.where(head0, s0, s1) * (1.0 / NA_HD) + EPS)

    qn = headnorm(q_ref[0].astype(F32)) * qg_ref[...] * (NA_HD ** -0.5)
    kn_ref[...] = (headnorm(k_ref[0, pl.ds(start, nwin), :].astype(F32)) * kg_ref[...]).astype(BF16)
    kcn = (headnorm(kc_ref[0].astype(F32)) * kg_ref[...]).astype(BF16)
    vc2 = vc_ref[0]
    dn = (((1,), (1,)), ((), ()))
    qh = [jnp.where(head0, qn, 0.0), jnp.where(head0, 0.0, qn)]
    s_ctx = [lax.dot_general(q.astype(BF16), kcn, dn, preferred_element_type=F32) for q in qh]
    for g, (qchunks, kchunks) in enumerate(ATT_GROUPS):
        ct = 0 if g == 0 else 1
        qrows = [(qr * GRID_W + c, w) for qr in range(QROWS) for (c, w) in qchunks]
        krows = [(kr * GRID_W + c, w) for kr in range(WROWS) for (c, w) in kchunks]
        kg = jnp.concatenate([kn_ref[a:a + w, :] for a, w in krows], axis=0)
        vg = jnp.concatenate([v_ref[0, pl.ds(pl.multiple_of(start + a, 16), w), :] for a, w in krows], axis=0)
        qg = jnp.concatenate([qh[hh][a:a + w] for hh in range(2) for a, w in qrows], axis=0).astype(BF16)
        sc = jnp.concatenate([s_ctx[hh][a:a + w] for hh in range(2) for a, w in qrows], axis=0)
        tab = jnp.concatenate([tab_ref[0, ct * 3 + ty], tab_ref[1, ct * 3 + ty]], axis=0)
        s = lax.dot_general(qg, kg, dn, preferred_element_type=F32) + tab
        m = jnp.maximum(jnp.max(s, axis=-1, keepdims=True), jnp.max(sc, axis=-1, keepdims=True))
        p = jnp.exp(s - m)
        pc = jnp.exp(sc - m)
        den = jnp.sum(p, axis=-1, keepdims=True) + jnp.sum(pc, axis=-1, keepdims=True)
        o = (jnp.dot(p.astype(BF16), vg, preferred_element_type=F32)
             + jnp.dot(pc.astype(BF16), vc2, preferred_element_type=F32)) / den
        nq = QROWS * ATT_QG
        og = jnp.where(head0, o[:nq], o[nq:])
        r = 0
        for a, w in qrows:
            o_scr[a:a + w, :] = og[r:r + w]
            r += w
    o_ref[0] = o_scr[...].astype(o_ref.dtype)


def _attn_latent(px, pcx, kc_cb, vc_cb, qg, kg, table):
    b, s, _ = px.shape
    c = pcx.shape[1]
    tq = QROWS * GRID_W
    nj = s // tq
    nwin = WROWS * GRID_W
    nhp = NA_HEADS // 2
    q_cb, k_cb, v_cb = OFF_Q // LANES, OFF_K // LANES, OFF_V // LANES
    gain = lambda t: jnp.tile(t.reshape(1, NA_HD), (1, 2))
    return pl.pallas_call(
        _attn_latent_kernel,
        grid=(b, nhp, nj),
        in_specs=[pl.BlockSpec((1, tq, LANES), lambda bi, hp, j: (bi, j, q_cb + hp)),
                  pl.BlockSpec((1, s, LANES), lambda bi, hp, j: (bi, 0, k_cb + hp)),
                  pl.BlockSpec((1, s, LANES), lambda bi, hp, j: (bi, 0, v_cb + hp)),
                  pl.BlockSpec((1, c, LANES), lambda bi, hp, j: (bi, 0, kc_cb + hp)),
                  pl.BlockSpec((1, c, LANES), lambda bi, hp, j: (bi, 0, vc_cb + hp)),
                  pl.BlockSpec((1, LANES), lambda bi, hp, j: (0, 0)),
                  pl.BlockSpec((1, LANES), lambda bi, hp, j: (0, 0)),
                  pl.BlockSpec((2, 6, QROWS * ATT_QG, WROWS * ATT_KG), lambda bi, hp, j: (hp, 0, 0, 0))],
        out_specs=pl.BlockSpec((1, tq, LANES), lambda bi, hp, j: (bi, j, hp)),
        out_shape=jax.ShapeDtypeStruct((b, s, D_NA), BF16),
        scratch_shapes=[pltpu.VMEM((nwin, LANES), BF16), pltpu.VMEM((tq, LANES), F32)],
        compiler_params=_cparams(("parallel", "parallel", "arbitrary")),
        name="attn_latent",
    )(px, px, px, pcx, pcx, gain(qg), gain(kg), table)


def _attn_ctx_kernel(q_ref, k_ref, v_ref, qg_ref, kg_ref, o_ref):
    qn = (_rms(q_ref[0, 0].astype(F32)) * qg_ref[...] * (NA_HD ** -0.5)).astype(BF16)
    kn = (_rms(k_ref[0, 0].astype(F32)) * kg_ref[...]).astype(BF16)
    s = lax.dot_general(qn, kn, (((1,), (1,)), ((), ())), preferred_element_type=F32)
    m = jnp.max(s, axis=-1, keepdims=True)
    p = jnp.exp(s - m)
    den = jnp.sum(p, axis=-1, keepdims=True)
    o = jnp.dot(p.astype(BF16), v_ref[0, 0], preferred_element_type=F32)
    o_ref[0, 0] = (o / den).astype(o_ref.dtype)


def _attn_ctx(q, k, v, qg, kg):
    b, h, c, hd = q.shape
    spec = pl.BlockSpec((1, 1, c, hd), lambda bi, hi: (bi, hi, 0, 0))
    gspec = pl.BlockSpec((1, hd), lambda bi, hi: (0, 0))
    return pl.pallas_call(
        _attn_ctx_kernel,
        grid=(b, h),
        in_specs=[spec, spec, spec, gspec, gspec],
        out_specs=spec,
        out_shape=jax.ShapeDtypeStruct((b, h, c, hd), BF16),
        compiler_params=_cparams(("parallel", "parallel")),
        name="attn_ctx",
    )(q, k, v, qg.reshape(1, hd), kg.reshape(1, hd))


N_DR = 2 * NA_KH - 1
N_DC = 2 * NA_KW - 1
DR_SLOTS = 24
DC_PAD = 32


@functools.lru_cache(maxsize=None)
def _bias_constants():
    onehots, oks = [], []
    for qchunks, kchunks in ATT_GROUPS[:2]:
        qc = np.concatenate([np.arange(c, c + w) for c, w in qchunks])[:, None]
        kc = np.tile(np.concatenate([np.arange(c, c + w) for c, w in kchunks]), LANES // ATT_KG)[None, :]
        cs = np.clip(qc - NA_KW // 2, 0, GRID_W - NA_KW)
        ok = (kc >= cs) & (kc < cs + NA_KW)
        dc = np.where(ok, kc - qc + NA_KW - 1, DC_PAD - 1)
        onehots.append((np.arange(DC_PAD)[:, None, None] == dc[None]) & ok[None])
        oks.append(ok)
    onehot = np.stack(onehots, axis=1)
    colneg = np.where(np.stack(oks), 0.0, NEG).reshape(1, -1)
    rowneg = np.full((DR_SLOTS, 1), NEG)
    rowneg[1:N_DR + 1] = 0.0
    f = np.float32
    return onehot.reshape(DC_PAD, -1).astype(f), colneg.astype(f), rowneg.astype(f)


def _bias_cols_kernel(rpb_ref, oh_ref, colneg_ref, rowneg_ref, o_ref):
    sel = jnp.dot(rpb_ref[0], oh_ref[...], preferred_element_type=F32, precision=HIGHEST)
    o_ref[0] = sel + colneg_ref[...] + rowneg_ref[...]


def _bias_table_kernel(m_ref, o_ref):
    lane = lax.broadcasted_iota(jnp.int32, (ATT_QG, LANES), 1)
    nj = N_ROWS // QROWS
    per = LANES // ATT_KG
    for ct in range(2):
        for ty, j in enumerate((0, 1, nj - 1)):
            wr = min(max(QROWS * j - (WROWS - QROWS) // 2, 0), N_ROWS - WROWS)
            for qr in range(QROWS):
                r = QROWS * j + qr
                rs = min(max(r - NA_KH // 2, 0), N_ROWS - NA_KH)
                slots = [kr - r + NA_KH if rs <= kr < rs + NA_KH else 0 for kr in range(wr, wr + WROWS)]
                for kq in range(WROWS // per):
                    e = slots[per * kq:per * (kq + 1)]
                    blk = m_ref[0, e[per - 1], ct]
                    for i in range(per - 2, -1, -1):
                        blk = jnp.where(lane < ATT_KG * (i + 1), m_ref[0, e[i], ct], blk)
                    o_ref[0, ct * 3 + ty, qr * ATT_QG:(qr + 1) * ATT_QG, kq * LANES:(kq + 1) * LANES] = blk


def _bias_table(rpb):
    h = rpb.shape[0]
    onehot, colneg, rowneg = _bias_constants()
    rp = jnp.zeros((h, DR_SLOTS, DC_PAD), F32).at[:, 1:N_DR + 1, :N_DC].set(rpb)
    ncol = 2 * ATT_QG * LANES
    cols = pl.pallas_call(
        _bias_cols_kernel,
        grid=(h,),
        in_specs=[pl.BlockSpec((1, DR_SLOTS, DC_PAD), lambda hi: (hi, 0, 0)),
                  pl.BlockSpec((DC_PAD, ncol), lambda hi: (0, 0)),
                  pl.BlockSpec((1, ncol), lambda hi: (0, 0)),
                  pl.BlockSpec((DR_SLOTS, 1), lambda hi: (0, 0))],
        out_specs=pl.BlockSpec((1, DR_SLOTS, ncol), lambda hi: (hi, 0, 0)),
        out_shape=jax.ShapeDtypeStruct((h, DR_SLOTS, ncol), F32),
        compiler_params=_cparams(("parallel",)),
        name="bias_cols",
    )(rp, jnp.asarray(onehot), jnp.asarray(colneg), jnp.asarray(rowneg))
    tqg, nkg = QROWS * ATT_QG, WROWS * ATT_KG
    return pl.pallas_call(
        _bias_table_kernel,
        grid=(h,),
        in_specs=[pl.BlockSpec((1, DR_SLOTS, 2, ATT_QG, LANES), lambda hi: (hi, 0, 0, 0, 0))],
        out_specs=pl.BlockSpec((1, 6, tqg, nkg), lambda hi: (hi, 0, 0, 0)),
        out_shape=jax.ShapeDtypeStruct((h, 6, tqg, nkg), F32),
        compiler_params=_cparams(("parallel",)),
        name="bias_table",
    )(cols.reshape(h, DR_SLOTS, 2, ATT_QG, LANES))


def _col_specs(off, width, tm, s):
    return [pl.BlockSpec((1, tm, CB), functools.partial(lambda bi, i, cb: (bi, i, cb), cb=off // CB + t))
            for t in range(width // CB)]


def _halo_specs(off, width, tm, s):
    r = tm // HALO
    last = s // HALO - 1
    specs = []
    for t in range(width // CB):
        cb = off // CB + t
        specs.append(pl.BlockSpec((1, HALO, CB),
                                  functools.partial(lambda bi, i, cb: (bi, jnp.maximum(i * r - 1, 0), cb), cb=cb)))
        specs.append(pl.BlockSpec((1, HALO, CB),
                                  functools.partial(lambda bi, i, cb: (bi, jnp.minimum((i + 1) * r, last), cb), cb=cb)))
    return specs


def _extend(main, prev, nxt, first, last):
    prev = jnp.where(first, 0.0, prev.astype(F32))
    nxt = jnp.where(last, 0.0, nxt.astype(F32))
    return jnp.concatenate([prev, main.astype(F32), nxt], axis=0)


def _dwconv_ext(ext, w, tm):
    k = w.shape[0]
    p = (k - 1) // 2
    n = ext.shape[0]
    acc = None
    for r in range(SUBLANES):
        inner = None
        for t in range(k):
            d = t - p
            if d % SUBLANES != r:
                continue
            a8 = d - r
            sh = ext if a8 == 0 else pltpu.roll(ext, (-a8) % n, 0)
            term = sh * w[t:t + 1, :]
            inner = term if inner is None else inner + term
        if inner is None:
            continue
        part = inner if r == 0 else pltpu.roll(inner, n - r, 0)
        acc = part if acc is None else acc + part
    return acc[HALO:HALO + tm]


def _hy_pre_kernel(*refs, tm):
    mains = refs[0:3]
    halos = refs[3:9]
    w_ref = refs[9]
    x0_ref, u_ref = refs[10], refs[11]
    i = pl.program_id(1)
    first = i == 0
    last = i == pl.num_programs(1) - 1
    outs = []
    for t in range(3):
        ext = _extend(mains[t][0], halos[2 * t][0], halos[2 * t + 1][0], first, last)
        outs.append(_dwconv_ext(ext, w_ref[:, t * CB:(t + 1) * CB], tm))
    x0_ref[0] = outs[0].astype(x0_ref.dtype)
    u_ref[0] = outs[1] * outs[2]


def _hy_pre(px, conv_h, tm):
    b, s, _ = px.shape
    in_specs = (_col_specs(OFF_HY, 3 * D_HY, tm, s) + _halo_specs(OFF_HY, 3 * D_HY, tm, s)
                + [pl.BlockSpec((SC_K, 3 * D_HY), lambda bi, i: (0, 0))])
    ospec = pl.BlockSpec((1, tm, D_HY), lambda bi, i: (bi, i, 0))
    return pl.pallas_call(
        functools.partial(_hy_pre_kernel, tm=tm),
        grid=(b, s // tm),
        in_specs=in_specs,
        out_specs=[ospec, ospec],
        out_shape=[jax.ShapeDtypeStruct((b, s, D_HY), BF16), jax.ShapeDtypeStruct((b, s, D_HY), F32)],
        compiler_params=_cparams(("parallel", "parallel")),
        name="hyena_pre",
    )(*([px] * 9), conv_h)


@functools.lru_cache(maxsize=None)
def _filter_features(length):
    t = np.linspace(0.0, 1.0, length)[:, None]
    bands = (HY_EMB - 1) // 2
    w = 2.0 * math.pi * np.arange(length)[:, None] / length
    f = np.linspace(1e-4, bands - 1, bands)[None, :]
    z = np.concatenate([t, np.cos(f * w), -np.sin(f * w)], axis=-1)
    pos = np.concatenate([np.arange(length), [0], np.arange(length - 1, 0, -1)])
    zp = np.zeros((2 * length, HY_FH), np.float32)
    zp[:, :HY_EMB] = z[pos]
    zp[:, HY_EMB] = 1.0
    zp[length, HY_EMB] = 0.0
    tl = _filter_tile(length)
    zp2 = zp.reshape(length // tl, 2, tl, HY_FH).transpose(0, 2, 1, 3).reshape(length, 2 * HY_FH)
    deltas = np.abs(np.linspace(math.log(HY_TARGET) / HY_SLOW, math.log(HY_TARGET) / HY_FAST, D_HY))
    return zp2, np.tile(deltas, 2)[None, :].astype(np.float32)


def _filter_tile(length):
    return min(length, 1024) // 2


def _filter_kernel(z_ref, w1_ref, b1_ref, w2_ref, b2_ref, w3_ref, b3_ref, w4a_ref, w4b_ref, fr_ref, dl_ref, o_ref):
    tl = z_ref.shape[0]
    z = z_ref[...]
    fr = fr_ref[...]
    dot = functools.partial(jnp.dot, preferred_element_type=F32, precision=HIGHEST)
    h = jnp.sin(fr * (dot(z, w1_ref[...]) + b1_ref[...]))
    h = jnp.sin(fr * (dot(h, w2_ref[...]) + b2_ref[...]))
    h = jnp.sin(fr * (dot(h, w3_ref[...]) + b3_ref[...]))
    dl = dl_ref[...]
    for part, w4_ref in enumerate((w4a_ref, w4b_ref)):
        c0 = part * HY_FH
        window = jnp.exp(-z[:, c0:c0 + 1] * dl) + HY_SHIFT
        o_ref[part * tl:(part + 1) * tl, :] = dot(h, w4_ref[...]) * window * z[:, c0 + HY_EMB:c0 + HY_EMB + 1]


def _hyena_conv_kernel(length, w1, b1, w2, b2, w3, b3, w4, freq):
    zp2, deltas = _filter_features(length)
    fh = HY_FH
    zero = jnp.zeros((fh, fh), F32)
    diag2 = lambda m: jnp.concatenate([jnp.concatenate([m, zero], 1), jnp.concatenate([zero, m], 1)], 0)
    twice = lambda v: jnp.tile(v.reshape(1, -1), (1, 2))
    w1p = jnp.zeros((fh, fh), F32).at[:HY_EMB].set(w1)
    w4z = jnp.zeros_like(w4)
    w4a = jnp.concatenate([w4, w4z], axis=0)
    w4b = jnp.concatenate([w4z, w4], axis=0)
    tl = _filter_tile(length)
    half = length // (2 * tl)
    full = lambda shape: pl.BlockSpec(shape, lambda i: (0, 0))
    w4_spec = pl.BlockSpec((2 * fh, D_HY), lambda i: (0, i // half))
    return pl.pallas_call(
        _filter_kernel,
        grid=(2 * half,),
        in_specs=[pl.BlockSpec((tl, 2 * fh), lambda i: (i, 0)),
                  full((2 * fh, 2 * fh)), full((1, 2 * fh)), full((2 * fh, 2 * fh)), full((1, 2 * fh)),
                  full((2 * fh, 2 * fh)), full((1, 2 * fh)),
                  w4_spec, w4_spec, full((1, 2 * fh)),
                  pl.BlockSpec((1, D_HY), lambda i: (0, i // half))],
        out_specs=pl.BlockSpec((2 * tl, D_HY), lambda i: (i, 0)),
        out_shape=jax.ShapeDtypeStruct((2 * length, D_HY), F32),
        compiler_params=_cparams(("parallel",)),
        name="hyena_filter",
    )(jnp.asarray(zp2), diag2(w1p), twice(b1), diag2(w2), twice(b2), diag2(w3), twice(b3), w4a, w4b,
      twice(freq), jnp.asarray(deltas))


FFT_N1 = 64
FFT_N2 = 128
FFT_N = FFT_N1 * FFT_N2


@functools.lru_cache(maxsize=None)
def _fft_constants():
    n1, n2, n = FFT_N1, FFT_N2, FFT_N
    k1 = np.arange(n1)[:, None]
    t1 = np.arange(n1)[None, :]
    a1 = 2.0 * math.pi * k1 * t1 / n1
    c1, s1 = np.cos(a1), np.sin(a1)
    h = n1 // 2
    w_data = np.zeros((n1, 2, 2 * h))
    w_data[:, 0, :h], w_data[:, 0, h:] = c1[:, :h], s1[:, :h]
    w_data[:, 1, :h], w_data[:, 1, h:] = -s1[:, :h], c1[:, :h]
    w_kern = np.stack([c1, -s1], axis=1)
    w_inv = np.zeros((2, h, n1, 2))
    w_inv[0, :, :, 0], w_inv[0, :, :, 1] = c1[:, :h].T, -s1[:, :h].T
    w_inv[1, :, :, 0], w_inv[1, :, :, 1] = s1[:, :h].T, c1[:, :h].T
    w_inv /= n
    t2 = np.arange(n2)[None, :]
    at = 2.0 * math.pi * k1 * t2 / n
    twr, twi = np.cos(at), -np.sin(at)
    k2 = np.arange(n2)[:, None]
    a2 = 2.0 * math.pi * k2 * t2 / n2
    c2, s2 = np.cos(a2), np.sin(a2)
    w2f = np.block([[c2, s2], [-s2, c2]])
    w2i = np.block([[c2, -s2], [s2, c2]])
    f = np.float32
    return dict(w_data=w_data.reshape(2 * n1, 2 * h).astype(f), w_kern=w_kern.reshape(2 * n1, n1).astype(f),
                w_inv=w_inv.reshape(2 * h, 2 * n1).astype(f), twr=twr[:, :, None].astype(f),
                twi=twi[:, :, None].astype(f), w2f=w2f.astype(f), w2i=w2i.astype(f))


DFT_SPLITS = 1
DFT_STORE = BF16 if DFT_SPLITS == 1 else F32
MID_KB = 4


def _split(x):
    x = x.astype(F32)
    hi = x.astype(BF16)
    if DFT_SPLITS == 1:
        return [hi]
    return [hi, (x - hi.astype(F32)).astype(BF16)]


def _dft_dot(a, b):
    acc = None
    for i, ai in enumerate(_split(a)):
        for j, bj in enumerate(_split(b)):
            if i + j < DFT_SPLITS:
                t = jnp.dot(ai, bj, preferred_element_type=F32)
                acc = t if acc is None else acc + t
    return acc


DFT_TB = 16
DFT_PITCH = 24
VLANES = 128


def _lmm_kernel(w_ref, z_ref, o_ref, zin_ref, out_ref):
    k, tb, ch = z_ref.shape[1:]
    m = o_ref.shape[1]
    nl = ch // VLANES
    pitch = DFT_PITCH
    for c in range(nl):
        lanes = slice(c * VLANES, (c + 1) * VLANES)
        for r in range(k):
            zin_ref[c, r * pitch:r * pitch + tb, :] = z_ref[0, r, :, lanes].astype(F32)
    w = w_ref[...]
    for t in range(tb):
        zt = jnp.concatenate([zin_ref[c, pl.ds(t, k, stride=pitch), :] for c in range(nl)], axis=-1)
        res = _dft_dot(w, zt)
        for c in range(nl):
            out_ref[c, pl.ds(t, m, stride=pitch), :] = res[:, c * VLANES:(c + 1) * VLANES]
    for c in range(nl):
        lanes = slice(c * VLANES, (c + 1) * VLANES)
        for r in range(m):
            o_ref[0, r, :, lanes] = out_ref[c, r * pitch:r * pitch + tb, :].astype(o_ref.dtype)


def _left_matmul(w, z, out_dtype):
    g, k, nt, ch = z.shape
    m = w.shape[0]
    tb = DFT_TB
    return pl.pallas_call(
        _lmm_kernel,
        grid=(g, nt // tb),
        in_specs=[pl.BlockSpec((m, k), lambda gi, j: (0, 0)),
                  pl.BlockSpec((1, k, tb, ch), lambda gi, j: (gi, 0, j, 0))],
        out_specs=pl.BlockSpec((1, m, tb, ch), lambda gi, j: (gi, 0, j, 0)),
        out_shape=jax.ShapeDtypeStruct((g, m, nt, ch), out_dtype),
        scratch_shapes=[pltpu.VMEM((ch // VLANES, k * DFT_PITCH, VLANES), F32),
                        pltpu.VMEM((ch // VLANES, m * DFT_PITCH, VLANES), F32)],
        compiler_params=_cparams(("parallel", "parallel")),
        name="dft_stage1",
    )(w, z)


def _twiddle_fwd(a, twr, twi):
    n2 = FFT_N2
    ar, ai = a[:n2].astype(F32), a[n2:].astype(F32)
    return ar * twr - ai * twi, ar * twi + ai * twr


def _stage2(w_ref, re, im):
    n2 = FFT_N2
    out = _dft_dot(w_ref[:, :n2], re) + _dft_dot(w_ref[:, n2:], im)
    return out[:n2], out[n2:]


def _kspec_kernel(a_ref, twr_ref, twi_ref, w2f_ref, o_ref):
    for t in range(MID_KB):
        br, bi = _twiddle_fwd(a_ref[0, t], twr_ref[t], twi_ref[t])
        xr, xi = _stage2(w2f_ref, br, bi)
        o_ref[t, :FFT_N2, :] = xr
        o_ref[t, FFT_N2:, :] = xi


def _fft_mid_kernel(a_ref, kf_ref, twr_ref, twi_ref, w2f_ref, w2i_ref, o_ref):
    n2 = FFT_N2
    for t in range(MID_KB):
        twr, twi = twr_ref[t], twi_ref[t]
        br, bi = _twiddle_fwd(a_ref[0, t], twr, twi)
        xr, xi = _stage2(w2f_ref, br, bi)
        kr, ki = kf_ref[t, :n2, :], kf_ref[t, n2:, :]
        yr = xr * kr - xi * ki
        yi = xr * ki + xi * kr
        cr, ci = _stage2(w2i_ref, yr, yi)
        o_ref[0, t, :n2, :] = (cr * twr + ci * twi).astype(o_ref.dtype)
        o_ref[0, t, n2:, :] = (ci * twr - cr * twi).astype(o_ref.dtype)


def _long_conv_latent(u, kern):
    b, l, ch = u.shape
    n1, n2 = FFT_N1, FFT_N2
    cst = _fft_constants()
    cols = n2 * ch
    kb = MID_KB
    tw_spec = pl.BlockSpec((kb, n2, 1), lambda k1, p: (k1, 0, 0))
    w2_spec = pl.BlockSpec((2 * n2, 2 * n2), lambda k1, p: (0, 0))
    ak = _left_matmul(jnp.asarray(cst["w_kern"]), kern.reshape(1, n1, n2, ch), DFT_STORE)
    kf = pl.pallas_call(
        _kspec_kernel,
        grid=(n1 // kb, 1),
        in_specs=[pl.BlockSpec((1, kb, 2 * n2, ch), lambda k1, p: (0, k1, 0, 0)), tw_spec, tw_spec, w2_spec],
        out_specs=pl.BlockSpec((kb, 2 * n2, ch), lambda k1, p: (k1, 0, 0)),
        out_shape=jax.ShapeDtypeStruct((n1, 2 * n2, ch), F32),
        compiler_params=_cparams(("parallel", "arbitrary")),
        name="dft_kernel_spectrum",
    )(ak.reshape(1, n1, 2 * n2, ch), jnp.asarray(cst["twr"]), jnp.asarray(cst["twi"]), jnp.asarray(cst["w2f"]))
    npair = b // 2
    a = _left_matmul(jnp.asarray(cst["w_data"]), u.reshape(npair, n1, n2, ch), DFT_STORE)
    d = pl.pallas_call(
        _fft_mid_kernel,
        grid=(n1 // kb, npair),
        in_specs=[pl.BlockSpec((1, kb, 2 * n2, ch), lambda k1, p: (p, k1, 0, 0)),
                  pl.BlockSpec((kb, 2 * n2, ch), lambda k1, p: (k1, 0, 0)),
                  tw_spec, tw_spec, w2_spec, w2_spec],
        out_specs=pl.BlockSpec((1, kb, 2 * n2, ch), lambda k1, p: (p, k1, 0, 0)),
        out_shape=jax.ShapeDtypeStruct((npair, n1, 2 * n2, ch), DFT_STORE),
        compiler_params=_cparams(("parallel", "arbitrary")),
        name="dft_mid",
    )(a.reshape(npair, n1, 2 * n2, ch), kf, jnp.asarray(cst["twr"]), jnp.asarray(cst["twi"]),
      jnp.asarray(cst["w2f"]), jnp.asarray(cst["w2i"]))
    y = _left_matmul(jnp.asarray(cst["w_inv"]), d.reshape(npair, 2 * n1, n2, ch), F32)
    return y.reshape(b, l, ch)


@functools.lru_cache(maxsize=None)
def _dft_constants(n):
    k = np.arange(n)[:, None]
    t = np.arange(n)[None, :]
    ang = 2.0 * math.pi * k * t / n
    return np.cos(ang).astype(np.float32), (-np.sin(ang)).astype(np.float32)


def _conv_ctx_kernel(u_ref, kern_ref, fr_ref, fi_ref, o_ref):
    l = u_ref.shape[1]
    n = 2 * l
    dot = functools.partial(jnp.dot, preferred_element_type=F32, precision=HIGHEST)
    fr, fi = fr_ref[...], fi_ref[...]
    kern = kern_ref[...]
    kr, ki = dot(fr, kern), dot(fi, kern)
    u = u_ref[0]
    ur, ui = dot(fr[:, :l], u), dot(fi[:, :l], u)
    yr = ur * kr - ui * ki
    yi = ur * ki + ui * kr
    o_ref[0] = (dot(fr[:l, :], yr) + dot(fi[:l, :], yi)) * (1.0 / n)


def _long_conv_ctx(u, kern):
    b, l, ch = u.shape
    n = 2 * l
    fr, fi = _dft_constants(n)
    return pl.pallas_call(
        _conv_ctx_kernel,
        grid=(b,),
        in_specs=[pl.BlockSpec((1, l, ch), lambda bi: (bi, 0, 0)),
                  pl.BlockSpec((n, ch), lambda bi: (0, 0)),
                  pl.BlockSpec((n, n), lambda bi: (0, 0)),
                  pl.BlockSpec((n, n), lambda bi: (0, 0))],
        out_specs=pl.BlockSpec((1, l, ch), lambda bi: (bi, 0, 0)),
        out_shape=jax.ShapeDtypeStruct((b, l, ch), F32),
        compiler_params=_cparams(("parallel",)),
        name="long_conv_ctx",
    )(u, kern, jnp.asarray(fr), jnp.asarray(fi))


def _mix_kernel(*refs, tm):
    it = iter(refs)
    x_ref, g1_ref = next(it), next(it)
    pa = [next(it) for _ in range(3)]
    pa_h = [next(it) for _ in range(6)]
    attn_ref, x0_ref, u_ref, yh_ref = next(it), next(it), next(it), next(it)
    pcf = [next(it) for _ in range(2)]
    pcf_h = [next(it) for _ in range(4)]
    pg = [next(it) for _ in range(8)]
    (conva_ref, wa_ref, wna_ref, skip_ref, wh_ref, convd_ref, convdb_ref, lng_ref, lnb_ref, wd_ref,
     wo_ref, o_ref) = [next(it) for _ in range(12)]

    i = pl.program_id(1)
    first = i == 0
    last = i == pl.num_programs(1) - 1
    dot = functools.partial(jnp.dot, preferred_element_type=F32)

    ext_x = _extend(pa[0][0], pa_h[0][0], pa_h[1][0], first, last)
    ext_c = _extend(pa[2][0], pa_h[4][0], pa_h[5][0], first, last)
    za = pa[1][0].astype(F32) * _dwconv_ext(ext_c * ext_x, conva_ref[...], tm)
    y_a = dot(za.astype(BF16), wa_ref[...])
    y_na = dot(attn_ref[0], wna_ref[...])
    u = u_ref[0]
    zh = x0_ref[0].astype(F32) * (yh_ref[0] + u * skip_ref[...])
    y_hy = dot(zh.astype(BF16), wh_ref[...])
    ext_a = _extend(pcf[0][0], pcf_h[0][0], pcf_h[1][0], first, last)
    ext_g = _extend(pcf[1][0], pcf_h[2][0], pcf_h[3][0], first, last)
    uc = _dwconv_ext(ext_a * ext_g, convd_ref[...], tm) + convdb_ref[...]
    mu = jnp.mean(uc, axis=-1, keepdims=True)
    dv = uc - mu
    var = jnp.mean(dv * dv, axis=-1, keepdims=True)
    zc = _silu(dv * lax.rsqrt(var + EPS) * lng_ref[...] + lnb_ref[...])
    y_cf = dot(zc.astype(BF16), wd_ref[...])

    halves = []
    for hf in range(2):
        sl = slice(hf * CB, (hf + 1) * CB)
        halves.append(pg[0 + hf][0] * y_a[:, sl].astype(BF16)
                      + pg[2 + hf][0] * y_na[:, sl].astype(BF16)
                      + pg[4 + hf][0] * y_hy[:, sl].astype(BF16)
                      + pg[6 + hf][0] * y_cf[:, sl].astype(BF16))
    merged = jnp.concatenate(halves, axis=-1)
    o_ref[0] = x_ref[0] + g1_ref[0] * dot(merged, wo_ref[...])


def _mix(x, g1, px, attn, x0, u, yh, wts, tm):
    b, s, d = x.shape
    row = lambda width: pl.BlockSpec((1, tm, width), lambda bi, i: (bi, i, 0))
    full2 = lambda a: pl.BlockSpec(a.shape, lambda bi, i: (0, 0))
    in_specs = ([row(d), pl.BlockSpec((1, 1, d), lambda bi, i: (bi, 0, 0))]
                + _col_specs(OFF_A, 3 * D_A, tm, s) + _halo_specs(OFF_A, 3 * D_A, tm, s)
                + [row(D_NA), row(D_HY), row(D_HY), row(D_HY)]
                + _col_specs(OFF_CF, 2 * D_CF, tm, s) + _halo_specs(OFF_CF, 2 * D_CF, tm, s)
                + _col_specs(OFF_G, N_BRANCH * D_MODEL, tm, s)
                + [full2(w) for w in wts])
    n_px = 3 + 6 + 2 + 4 + 8
    args = [x, g1] + [px] * 9 + [attn, x0, u, yh] + [px] * 14 + list(wts)
    assert len(args) == len(in_specs) and n_px == 23
    return pl.pallas_call(
        functools.partial(_mix_kernel, tm=tm),
        grid=(b, s // tm),
        in_specs=in_specs,
        out_specs=row(d),
        out_shape=jax.ShapeDtypeStruct((b, s, d), F32),
        compiler_params=_cparams(("parallel", "parallel")),
        name="mixer_out",
    )(*args)


FFN_CHUNK = 256


def _ffn_kernel(x_ref, g_ref, sh_ref, sc_ref, gate_ref, wa_ref, wu_ref, wo_ref, o_ref):
    x = x_ref[0]
    h = _mod_rmsnorm(x, g_ref[...], sh_ref[0], sc_ref[0]).astype(BF16)
    acc = None
    for c in range(wo_ref.shape[0] // FFN_CHUNK):
        cols = slice(c * FFN_CHUNK, (c + 1) * FFN_CHUNK)
        a = jnp.dot(h, wa_ref[:, cols], preferred_element_type=F32)
        up = jnp.dot(h, wu_ref[:, cols], preferred_element_type=F32)
        act = (_silu(a) * up).astype(BF16)
        part = jnp.dot(act, wo_ref[cols, :], preferred_element_type=F32)
        acc = part if acc is None else acc + part
    o_ref[0] = x + gate_ref[0] * acc


def _ffn(x, g, sh, sc, gate, w_in, w_out, tm):
    b, s, d = x.shape
    dff = w_out.shape[0]
    vec = pl.BlockSpec((1, 1, d), lambda bi, i: (bi, 0, 0))
    once = pl.Buffered(1)
    return pl.pallas_call(
        _ffn_kernel,
        grid=(b, s // tm),
        in_specs=[pl.BlockSpec((1, tm, d), lambda bi, i: (bi, i, 0)),
                  pl.BlockSpec((1, d), lambda bi, i: (0, 0)),
                  vec, vec, vec,
                  pl.BlockSpec((d, dff), lambda bi, i: (0, 0), pipeline_mode=once),
                  pl.BlockSpec((d, dff), lambda bi, i: (0, 1), pipeline_mode=once),
                  pl.BlockSpec((dff, d), lambda bi, i: (0, 0), pipeline_mode=once)],
        out_specs=pl.BlockSpec((1, tm, d), lambda bi, i: (bi, i, 0)),
        out_shape=jax.ShapeDtypeStruct((b, s, d), F32),
        compiler_params=_cparams(("parallel", "parallel")),
        name="ffn",
    )(x, g.reshape(1, d), sh, sc, gate, w_in, w_in, w_out)


def _heads(t):
    b, s, _ = t.shape
    return t.reshape(b, s, NA_HEADS, NA_HD).transpose(0, 2, 1, 3)


def _unheads(t):
    b, h, s, hd = t.shape
    return t.transpose(0, 2, 1, 3).reshape(b, s, h * hd)


def kernel(x, c, ctx, c_ctx, w_mod, b_mod, g_norm1, g_norm2, w_in, conv_a, w_a_out, q_gain, k_gain, rpb, w_na_out, conv_h, filt_w1, filt_b1, filt_w2, filt_b2, filt_w3, filt_b3, filt_w4, filt_freq, hy_skip, w_h_out, conv_d, conv_d_b, ln_g, ln_b, w_d_out, w_o, w_ffn_in, w_ffn_out):
    depth = w_mod.shape[0]
    b, s, d = x.shape
    n_ctx = ctx.shape[1]
    cc = jnp.zeros((8, d), F32).at[:b].set(c).at[b].set(c_ctx)

    for l in range(depth):
        last = l == depth - 1
        mods = _modulation(cc, w_mod[l], b_mod[l])
        lat = [mods[:b, t * d:(t + 1) * d].reshape(b, 1, d) for t in range(6)]
        cxm = [jnp.broadcast_to(mods[b:b + 1, t * d:(t + 1) * d].reshape(1, 1, d), (b, 1, d)) for t in range(6)]
        sh1, sc1, g1, sh2, sc2, g2 = lat
        csh1, csc1, cg1, csh2, csc2, cg2 = cxm

        w_in_b = w_in[l].astype(BF16)
        mix_w = (conv_a[l], w_a_out[l].astype(BF16), w_na_out[l].astype(BF16), hy_skip[l].reshape(1, -1),
                 w_h_out[l].astype(BF16), conv_d[l], conv_d_b[l].reshape(1, -1), ln_g[l].reshape(1, -1),
                 ln_b[l].reshape(1, -1), w_d_out[l].astype(BF16), w_o[l].astype(BF16))
        w_ffn_in_b = w_ffn_in[l].astype(BF16)
        w_ffn_out_b = w_ffn_out[l].astype(BF16)
        filt_args = (filt_w1[l], filt_b1[l], filt_w2[l], filt_b2[l], filt_w3[l], filt_b3[l], filt_w4[l], filt_freq[l])

        px = _in_proj(x, g_norm1[l], sh1, sc1, w_in_b, 512, SIG_FROM)
        if last:
            pcx = _in_proj(ctx, g_norm1[l], csh1, csc1, w_in_b[:, OFF_K:OFF_HY], n_ctx, 2 * D_NA // CB)
            kc_cb, vc_cb = 0, D_NA // LANES
        else:
            pc = pcx = _in_proj(ctx, g_norm1[l], csh1, csc1, w_in_b, n_ctx, SIG_FROM)
            kc_cb, vc_cb = OFF_K // LANES, OFF_V // LANES

        table = _bias_table(rpb[l])
        attn_x = _attn_latent(px, pcx, kc_cb, vc_cb, q_gain[l], k_gain[l], table)

        x0, u = _hy_pre(px, conv_h[l], 512)
        kern = _hyena_conv_kernel(s, *filt_args)
        yh = _long_conv_latent(u, kern)
        x_new = _mix(x, g1, px, attn_x, x0, u, yh, mix_w, 256)

        if not last:
            attn_c = _unheads(_attn_ctx(_heads(pc[..., OFF_Q:OFF_K]), _heads(pc[..., OFF_K:OFF_V]),
                                        _heads(pc[..., OFF_V:OFF_HY]), q_gain[l], k_gain[l]))
            x0c, uc = _hy_pre(pc, conv_h[l], n_ctx)
            kern_c = _hyena_conv_kernel(n_ctx, *filt_args)
            yhc = _long_conv_ctx(uc, kern_c)
            ctx = _mix(ctx, cg1, pc, attn_c, x0c, uc, yhc, mix_w, n_ctx)
            ctx = _ffn(ctx, g_norm2[l], csh2, csc2, cg2, w_ffn_in_b, w_ffn_out_b, n_ctx)

        x = _ffn(x_new, g_norm2[l], sh2, sc2, g2, w_ffn_in_b, w_ffn_out_b, 512)
    return x
```

```python
import functools
import math

import numpy as np
import jax
import jax.numpy as jnp
from jax import lax
from jax.experimental import pallas as pl
from jax.experimental.pallas import tpu as pltpu

F32 = jnp.float32
BF16 = jnp.bfloat16
HIGHEST = lax.Precision.HIGHEST

D_MODEL = 1024
GRID_W = 64
N_BRANCH = 4
D_A = D_MODEL // 2
SC_K = 3
D_NA = D_MODEL // 2
NA_HEADS = 8
NA_HD = D_NA // NA_HEADS
NA_KH = 8
NA_KW = 16
D_HY = D_MODEL // 2
HY_EMB = 33
HY_FH = 64
HY_SHIFT = 0.05
HY_FAST = 0.3
HY_SLOW = 1.5
HY_TARGET = 1e-2
D_CF = D_MODEL // 2
CF_K = 31
D_FF = ((8 * D_MODEL + 3 * 256 - 1) // (3 * 256)) * 256
EPS = 1e-6
OFF_A = 0
OFF_Q = OFF_A + 3 * D_A
OFF_K = OFF_Q + D_NA
OFF_V = OFF_K + D_NA
OFF_HY = OFF_V + D_NA
OFF_CF = OFF_HY + 3 * D_HY
OFF_G = OFF_CF + 2 * D_CF
N_IN = OFF_G + N_BRANCH * D_MODEL

CB = 512
HALO = 16
SUBLANES = 8
NEG = -1e30
VMEM_LIMIT = 56 * 1024 * 1024

QROWS = 8
WROWS = 16
N_ROWS = 64


def _cparams(sem):
    return pltpu.CompilerParams(dimension_semantics=sem, vmem_limit_bytes=VMEM_LIMIT)


def _sigmoid(x):
    return 1.0 / (1.0 + jnp.exp(-x))


def _silu(x):
    return x * _sigmoid(x)


def _rms(x):
    return x * lax.rsqrt(jnp.mean(x * x, axis=-1, keepdims=True) + EPS)


def _mod_rmsnorm(x, g, shift, scale):
    return (_rms(x) * g) * (1.0 + scale) + shift


def _mod_kernel(c_ref, w_ref, b_ref, o_ref):
    s = _silu(c_ref[...])
    o_ref[...] = jnp.dot(s, w_ref[...], preferred_element_type=F32, precision=HIGHEST) + b_ref[...]


def _modulation(cc, w, b):
    m, d = cc.shape
    n = w.shape[1]
    tn = 1536
    return pl.pallas_call(
        _mod_kernel,
        grid=(n // tn,),
        in_specs=[pl.BlockSpec((m, d), lambda j: (0, 0)),
                  pl.BlockSpec((d, tn), lambda j: (0, j)),
                  pl.BlockSpec((1, tn), lambda j: (0, j))],
        out_specs=pl.BlockSpec((m, tn), lambda j: (0, j)),
        out_shape=jax.ShapeDtypeStruct((m, n), F32),
        compiler_params=_cparams(("arbitrary",)),
        name="modulation",
    )(cc, w, b.reshape(1, n))


SIG_FROM = (OFF_CF + D_CF) // CB


def _in_proj_kernel(x_ref, g_ref, sh_ref, sc_ref, w_ref, o_ref, *, n_plain):
    h = _mod_rmsnorm(x_ref[0], g_ref[...], sh_ref[0], sc_ref[0]).astype(BF16)
    for c in range(w_ref.shape[1] // CB):
        cols = slice(c * CB, (c + 1) * CB)
        v = jnp.dot(h, w_ref[:, cols], preferred_element_type=F32)
        if c >= n_plain:
            v = _sigmoid(v)
        o_ref[0, :, cols] = v.astype(o_ref.dtype)


def _in_proj(x, g, sh, sc, w, tm, n_plain):
    b, s, d = x.shape
    n = w.shape[1]
    return pl.pallas_call(
        functools.partial(_in_proj_kernel, n_plain=n_plain),
        grid=(b, s // tm),
        in_specs=[pl.BlockSpec((1, tm, d), lambda bi, i: (bi, i, 0)),
                  pl.BlockSpec((1, d), lambda bi, i: (0, 0)),
                  pl.BlockSpec((1, 1, d), lambda bi, i: (bi, 0, 0)),
                  pl.BlockSpec((1, 1, d), lambda bi, i: (bi, 0, 0)),
                  pl.BlockSpec((d, n), lambda bi, i: (0, 0), pipeline_mode=pl.Buffered(1))],
        out_specs=pl.BlockSpec((1, tm, n), lambda bi, i: (bi, i, 0)),
        out_shape=jax.ShapeDtypeStruct((b, s, n), BF16),
        compiler_params=_cparams(("parallel", "parallel")),
        name="in_proj",
    )(x, g.reshape(1, d), sh, sc, w)


ATT_GROUPS = (
    (((0, 8), (56, 8)), ((0, 16), (48, 16))),
    (((8, 16),), ((0, 32),)),
    (((24, 16),), ((16, 32),)),
    (((40, 16),), ((32, 32),)),
)
ATT_QG = 16
ATT_KG = 32
LANES = 2 * NA_HD


def _attn_latent_kernel(q_ref, k_ref, v_ref, kc_ref, vc_ref, qg_ref, kg_ref, tab_ref, o_ref, kn_ref, o_scr):
    j = pl.program_id(2)
    nj = pl.num_programs(2)
    wr = jnp.clip(QROWS * j - (WROWS - QROWS) // 2, 0, N_ROWS - WROWS)
    start = pl.multiple_of(wr * GRID_W, 256)
    ty = jnp.where(j == 0, 0, jnp.where(j == nj - 1, 2, 1))
    nwin = WROWS * GRID_W
    head0 = lax.broadcasted_iota(jnp.int32, (1, LANES), 1) < NA_HD

    def headnorm(x):
        sq = x * x
        s0 = jnp.sum(jnp.where(head0, sq, 0.0), axis=-1, keepdims=True)
        s1 = jnp.sum(jnp.where(head0, 0.0, sq), axis=-1, keepdims=True)
        return x * lax.rsqrt(jnp.where(head0, s0, s1) * (1.0 / NA_HD) + EPS)

    qn = headnorm(q_ref[0].astype(F32)) * qg_ref[...] * (NA_HD ** -0.5)
    kn_ref[...] = (headnorm(k_ref[0, pl.ds(start, nwin), :].astype(F32)) * kg_ref[...]).astype(BF16)
    kcn = (headnorm(kc_ref[0].astype(F32)) * kg_ref[...]).astype(BF16)
    vc2 = vc_ref[0]
    dn = (((1,), (1,)), ((), ()))
    qh = [jnp.where(head0, qn, 0.0), jnp.where(head0, 0.0, qn)]
    s_ctx = [lax.dot_general(q.astype(BF16), kcn, dn, preferred_element_type=F32) for q in qh]
    for g, (qchunks, kchunks) in enumerate(ATT_GROUPS):
        ct = 0 if g == 0 else 1
        qrows = [(qr * GRID_W + c, w) for qr in range(QROWS) for (c, w) in qchunks]
        krows = [(kr * GRID_W + c, w) for kr in range(WROWS) for (c, w) in kchunks]
        kg = jnp.concatenate([kn_ref[a:a + w, :] for a, w in krows], axis=0)
        vg = jnp.concatenate([v_ref[0, pl.ds(pl.multiple_of(start + a, 16), w), :] for a, w in krows], axis=0)
        qg = jnp.concatenate([qh[hh][a:a + w] for hh in range(2) for a, w in qrows], axis=0).astype(BF16)
        sc = jnp.concatenate([s_ctx[hh][a:a + w] for hh in range(2) for a, w in qrows], axis=0)
        tab = jnp.concatenate([tab_ref[0, ct * 3 + ty], tab_ref[1, ct * 3 + ty]], axis=0)
        s = lax.dot_general(qg, kg, dn, preferred_element_type=F32) + tab
        m = jnp.maximum(jnp.max(s, axis=-1, keepdims=True), jnp.max(sc, axis=-1, keepdims=True))
        p = jnp.exp(s - m)
        pc = jnp.exp(sc - m)
        den = jnp.sum(p, axis=-1, keepdims=True) + jnp.sum(pc, axis=-1, keepdims=True)
        o = (jnp.dot(p.astype(BF16), vg, preferred_element_type=F32)
             + jnp.dot(pc.astype(BF16), vc2, preferred_element_type=F32)) / den
        nq = QROWS * ATT_QG
        og = jnp.where(head0, o[:nq], o[nq:])
        r = 0
        for a, w in qrows:
            o_scr[a:a + w, :] = og[r:r + w]
            r += w
    o_ref[0] = o_scr[...].astype(o_ref.dtype)


def _attn_latent(px, pcx, kc_cb, vc_cb, qg, kg, table):
    b, s, _ = px.shape
    c = pcx.shape[1]
    tq = QROWS * GRID_W
    nj = s // tq
    nwin = WROWS * GRID_W
    nhp = NA_HEADS // 2
    q_cb, k_cb, v_cb = OFF_Q // LANES, OFF_K // LANES, OFF_V // LANES
    gain = lambda t: jnp.tile(t.reshape(1, NA_HD), (1, 2))
    return pl.pallas_call(
        _attn_latent_kernel,
        grid=(b, nhp, nj),
        in_specs=[pl.BlockSpec((1, tq, LANES), lambda bi, hp, j: (bi, j, q_cb + hp)),
                  pl.BlockSpec((1, s, LANES), lambda bi, hp, j: (bi, 0, k_cb + hp)),
                  pl.BlockSpec((1, s, LANES), lambda bi, hp, j: (bi, 0, v_cb + hp)),
                  pl.BlockSpec((1, c, LANES), lambda bi, hp, j: (bi, 0, kc_cb + hp)),
                  pl.BlockSpec((1, c, LANES), lambda bi, hp, j: (bi, 0, vc_cb + hp)),
                  pl.BlockSpec((1, LANES), lambda bi, hp, j: (0, 0)),
                  pl.BlockSpec((1, LANES), lambda bi, hp, j: (0, 0)),
                  pl.BlockSpec((2, 6, QROWS * ATT_QG, WROWS * ATT_KG), lambda bi, hp, j: (hp, 0, 0, 0))],
        out_specs=pl.BlockSpec((1, tq, LANES), lambda bi, hp, j: (bi, j, hp)),
        out_shape=jax.ShapeDtypeStruct((b, s, D_NA), BF16),
        scratch_shapes=[pltpu.VMEM((nwin, LANES), BF16), pltpu.VMEM((tq, LANES), F32)],
        compiler_params=_cparams(("parallel", "parallel", "arbitrary")),
        name="attn_latent",
    )(px, px, px, pcx, pcx, gain(qg), gain(kg), table)


def _attn_ctx_kernel(q_ref, k_ref, v_ref, qg_ref, kg_ref, o_ref):
    qn = (_rms(q_ref[0, 0].astype(F32)) * qg_ref[...] * (NA_HD ** -0.5)).astype(BF16)
    kn = (_rms(k_ref[0, 0].astype(F32)) * kg_ref[...]).astype(BF16)
    s = lax.dot_general(qn, kn, (((1,), (1,)), ((), ())), preferred_element_type=F32)
    m = jnp.max(s, axis=-1, keepdims=True)
    p = jnp.exp(s - m)
    den = jnp.sum(p, axis=-1, keepdims=True)
    o = jnp.dot(p.astype(BF16), v_ref[0, 0], preferred_element_type=F32)
    o_ref[0, 0] = (o / den).astype(o_ref.dtype)


def _attn_ctx(q, k, v, qg, kg):
    b, h, c, hd = q.shape
    spec = pl.BlockSpec((1, 1, c, hd), lambda bi, hi: (bi, hi, 0, 0))
    gspec = pl.BlockSpec((1, hd), lambda bi, hi: (0, 0))
    return pl.pallas_call(
        _attn_ctx_kernel,
        grid=(b, h),
        in_specs=[spec, spec, spec, gspec, gspec],
        out_specs=spec,
        out_shape=jax.ShapeDtypeStruct((b, h, c, hd), BF16),
        compiler_params=_cparams(("parallel", "parallel")),
        name="attn_ctx",
    )(q, k, v, qg.reshape(1, hd), kg.reshape(1, hd))


N_DR = 2 * NA_KH - 1
N_DC = 2 * NA_KW - 1
DR_SLOTS = 24
DC_PAD = 32


@functools.lru_cache(maxsize=None)
def _bias_constants():
    onehots, oks = [], []
    for qchunks, kchunks in ATT_GROUPS[:2]:
        qc = np.concatenate([np.arange(c, c + w) for c, w in qchunks])[:, None]
        kc = np.tile(np.concatenate([np.arange(c, c + w) for c, w in kchunks]), LANES // ATT_KG)[None, :]
        cs = np.clip(qc - NA_KW // 2, 0, GRID_W - NA_KW)
        ok = (kc >= cs) & (kc < cs + NA_KW)
        dc = np.where(ok, kc - qc + NA_KW - 1, DC_PAD - 1)
        onehots.append((np.arange(DC_PAD)[:, None, None] == dc[None]) & ok[None])
        oks.append(ok)
    onehot = np.stack(onehots, axis=1)
    colneg = np.where(np.stack(oks), 0.0, NEG).reshape(1, -1)
    rowneg = np.full((DR_SLOTS, 1), NEG)
    rowneg[1:N_DR + 1] = 0.0
    f = np.float32
    return onehot.reshape(DC_PAD, -1).astype(f), colneg.astype(f), rowneg.astype(f)


def _bias_cols_kernel(rpb_ref, oh_ref, colneg_ref, rowneg_ref, o_ref):
    sel = jnp.dot(rpb_ref[0], oh_ref[...], preferred_element_type=F32, precision=HIGHEST)
    o_ref[0] = sel + colneg_ref[...] + rowneg_ref[...]


def _bias_table_kernel(m_ref, o_ref):
    lane = lax.broadcasted_iota(jnp.int32, (ATT_QG, LANES), 1)
    nj = N_ROWS // QROWS
    per = LANES // ATT_KG
    for ct in range(2):
        for ty, j in enumerate((0, 1, nj - 1)):
            wr = min(max(QROWS * j - (WROWS - QROWS) // 2, 0), N_ROWS - WROWS)
            for qr in range(QROWS):
                r = QROWS * j + qr
                rs = min(max(r - NA_KH // 2, 0), N_ROWS - NA_KH)
                slots = [kr - r + NA_KH if rs <= kr < rs + NA_KH else 0 for kr in range(wr, wr + WROWS)]
                for kq in range(WROWS // per):
                    e = slots[per * kq:per * (kq + 1)]
                    blk = m_ref[0, e[per - 1], ct]
                    for i in range(per - 2, -1, -1):
                        blk = jnp.where(lane < ATT_KG * (i + 1), m_ref[0, e[i], ct], blk)
                    o_ref[0, ct * 3 + ty, qr * ATT_QG:(qr + 1) * ATT_QG, kq * LANES:(kq + 1) * LANES] = blk


def _bias_table(rpb):
    h = rpb.shape[0]
    onehot, colneg, rowneg = _bias_constants()
    rp = jnp.zeros((h, DR_SLOTS, DC_PAD), F32).at[:, 1:N_DR + 1, :N_DC].set(rpb)
    ncol = 2 * ATT_QG * LANES
    cols = pl.pallas_call(
        _bias_cols_kernel,
        grid=(h,),
        in_specs=[pl.BlockSpec((1, DR_SLOTS, DC_PAD), lambda hi: (hi, 0, 0)),
                  pl.BlockSpec((DC_PAD, ncol), lambda hi: (0, 0)),
                  pl.BlockSpec((1, ncol), lambda hi: (0, 0)),
                  pl.BlockSpec((DR_SLOTS, 1), lambda hi: (0, 0))],
        out_specs=pl.BlockSpec((1, DR_SLOTS, ncol), lambda hi: (hi, 0, 0)),
        out_shape=jax.ShapeDtypeStruct((h, DR_SLOTS, ncol), F32),
        compiler_params=_cparams(("parallel",)),
        name="bias_cols",
    )(rp, jnp.asarray(onehot), jnp.asarray(colneg), jnp.asarray(rowneg))
    tqg, nkg = QROWS * ATT_QG, WROWS * ATT_KG
    return pl.pallas_call(
        _bias_table_kernel,
        grid=(h,),
        in_specs=[pl.BlockSpec((1, DR_SLOTS, 2, ATT_QG, LANES), lambda hi: (hi, 0, 0, 0, 0))],
        out_specs=pl.BlockSpec((1, 6, tqg, nkg), lambda hi: (hi, 0, 0, 0)),
        out_shape=jax.ShapeDtypeStruct((h, 6, tqg, nkg), F32),
        compiler_params=_cparams(("parallel",)),
        name="bias_table",
    )(cols.reshape(h, DR_SLOTS, 2, ATT_QG, LANES))


def _col_specs(off, width, tm, s):
    return [pl.BlockSpec((1, tm, CB), functools.partial(lambda bi, i, cb: (bi, i, cb), cb=off // CB + t))
            for t in range(width // CB)]


def _halo_specs(off, width, tm, s):
    r = tm // HALO
    last = s // HALO - 1
    specs = []
    for t in range(width // CB):
        cb = off // CB + t
        specs.append(pl.BlockSpec((1, HALO, CB),
                                  functools.partial(lambda bi, i, cb: (bi, jnp.maximum(i * r - 1, 0), cb), cb=cb)))
        specs.append(pl.BlockSpec((1, HALO, CB),
                                  functools.partial(lambda bi, i, cb: (bi, jnp.minimum((i + 1) * r, last), cb), cb=cb)))
    return specs


def _extend(main, prev, nxt, first, last):
    prev = jnp.where(first, 0.0, prev.astype(F32))
    nxt = jnp.where(last, 0.0, nxt.astype(F32))
    return jnp.concatenate([prev, main.astype(F32), nxt], axis=0)


def _dwconv_ext(ext, w, tm):
    k = w.shape[0]
    p = (k - 1) // 2
    n = ext.shape[0]
    acc = None
    for r in range(SUBLANES):
        inner = None
        for t in range(k):
            d = t - p
            if d % SUBLANES != r:
                continue
            a8 = d - r
            sh = ext if a8 == 0 else pltpu.roll(ext, (-a8) % n, 0)
            term = sh * w[t:t + 1, :]
            inner = term if inner is None else inner + term
        if inner is None:
            continue
        part = inner if r == 0 else pltpu.roll(inner, n - r, 0)
        acc = part if acc is None else acc + part
    return acc[HALO:HALO + tm]


def _hy_pre_kernel(*refs, tm):
    mains = refs[0:3]
    halos = refs[3:9]
    w_ref = refs[9]
    x0_ref, u_ref = refs[10], refs[11]
    i = pl.program_id(1)
    first = i == 0
    last = i == pl.num_programs(1) - 1
    outs = []
    for t in range(3):
        ext = _extend(mains[t][0], halos[2 * t][0], halos[2 * t + 1][0], first, last)
        outs.append(_dwconv_ext(ext, w_ref[:, t * CB:(t + 1) * CB], tm))
    x0_ref[0] = outs[0].astype(x0_ref.dtype)
    u_ref[0] = outs[1] * outs[2]


def _hy_pre(px, conv_h, tm):
    b, s, _ = px.shape
    in_specs = (_col_specs(OFF_HY, 3 * D_HY, tm, s) + _halo_specs(OFF_HY, 3 * D_HY, tm, s)
                + [pl.BlockSpec((SC_K, 3 * D_HY), lambda bi, i: (0, 0))])
    ospec = pl.BlockSpec((1, tm, D_HY), lambda bi, i: (bi, i, 0))
    return pl.pallas_call(
        functools.partial(_hy_pre_kernel, tm=tm),
        grid=(b, s // tm),
        in_specs=in_specs,
        out_specs=[ospec, ospec],
        out_shape=[jax.ShapeDtypeStruct((b, s, D_HY), BF16), jax.ShapeDtypeStruct((b, s, D_HY), F32)],
        compiler_params=_cparams(("parallel", "parallel")),
        name="hyena_pre",
    )(*([px] * 9), conv_h)


@functools.lru_cache(maxsize=None)
def _filter_features(length):
    t = np.linspace(0.0, 1.0, length)[:, None]
    bands = (HY_EMB - 1) // 2
    w = 2.0 * math.pi * np.arange(length)[:, None] / length
    f = np.linspace(1e-4, bands - 1, bands)[None, :]
    z = np.concatenate([t, np.cos(f * w), -np.sin(f * w)], axis=-1)
    pos = np.concatenate([np.arange(length), [0], np.arange(length - 1, 0, -1)])
    zp = np.zeros((2 * length, HY_FH), np.float32)
    zp[:, :HY_EMB] = z[pos]
    zp[:, HY_EMB] = 1.0
    zp[length, HY_EMB] = 0.0
    tl = _filter_tile(length)
    zp2 = zp.reshape(length // tl, 2, tl, HY_FH).transpose(0, 2, 1, 3).reshape(length, 2 * HY_FH)
    deltas = np.abs(np.linspace(math.log(HY_TARGET) / HY_SLOW, math.log(HY_TARGET) / HY_FAST, D_HY))
    return zp2, np.tile(deltas, 2)[None, :].astype(np.float32)


def _filter_tile(length):
    return min(length, 1024) // 2


def _filter_kernel(z_ref, w1_ref, b1_ref, w2_ref, b2_ref, w3_ref, b3_ref, w4a_ref, w4b_ref, fr_ref, dl_ref, o_ref):
    tl = z_ref.shape[0]
    z = z_ref[...]
    fr = fr_ref[...]
    dot = functools.partial(jnp.dot, preferred_element_type=F32, precision=HIGHEST)
    h = jnp.sin(fr * (dot(z, w1_ref[...]) + b1_ref[...]))
    h = jnp.sin(fr * (dot(h, w2_ref[...]) + b2_ref[...]))
    h = jnp.sin(fr * (dot(h, w3_ref[...]) + b3_ref[...]))
    dl = dl_ref[...]
    for part, w4_ref in enumerate((w4a_ref, w4b_ref)):
        c0 = part * HY_FH
        window = jnp.exp(-z[:, c0:c0 + 1] * dl) + HY_SHIFT
        o_ref[part * tl:(part + 1) * tl, :] = dot(h, w4_ref[...]) * window * z[:, c0 + HY_EMB:c0 + HY_EMB + 1]


def _hyena_conv_kernel(length, w1, b1, w2, b2, w3, b3, w4, freq):
    zp2, deltas = _filter_features(length)
    fh = HY_FH
    zero = jnp.zeros((fh, fh), F32)
    diag2 = lambda m: jnp.concatenate([jnp.concatenate([m, zero], 1), jnp.concatenate([zero, m], 1)], 0)
    twice = lambda v: jnp.tile(v.reshape(1, -1), (1, 2))
    w1p = jnp.zeros((fh, fh), F32).at[:HY_EMB].set(w1)
    w4z = jnp.zeros_like(w4)
    w4a = jnp.concatenate([w4, w4z], axis=0)
    w4b = jnp.concatenate([w4z, w4], axis=0)
    tl = _filter_tile(length)
    half = length // (2 * tl)
    full = lambda shape: pl.BlockSpec(shape, lambda i: (0, 0))
    w4_spec = pl.BlockSpec((2 * fh, D_HY), lambda i: (0, i // half))
    return pl.pallas_call(
        _filter_kernel,
        grid=(2 * half,),
        in_specs=[pl.BlockSpec((tl, 2 * fh), lambda i: (i, 0)),
                  full((2 * fh, 2 * fh)), full((1, 2 * fh)), full((2 * fh, 2 * fh)), full((1, 2 * fh)),
                  full((2 * fh, 2 * fh)), full((1, 2 * fh)),
                  w4_spec, w4_spec, full((1, 2 * fh)),
                  pl.BlockSpec((1, D_HY), lambda i: (0, i // half))],
        out_specs=pl.BlockSpec((2 * tl, D_HY), lambda i: (i, 0)),
        out_shape=jax.ShapeDtypeStruct((2 * length, D_HY), F32),
        compiler_params=_cparams(("parallel",)),
        name="hyena_filter",
    )(jnp.asarray(zp2), diag2(w1p), twice(b1), diag2(w2), twice(b2), diag2(w3), twice(b3), w4a, w4b,
      twice(freq), jnp.asarray(deltas))


FFT_N1 = 64
FFT_N2 = 128
FFT_N = FFT_N1 * FFT_N2


@functools.lru_cache(maxsize=None)
def _fft_constants():
    n1, n2, n = FFT_N1, FFT_N2, FFT_N
    k1 = np.arange(n1)[:, None]
    t1 = np.arange(n1)[None, :]
    a1 = 2.0 * math.pi * k1 * t1 / n1
    c1, s1 = np.cos(a1), np.sin(a1)
    h = n1 // 2
    w_data = np.zeros((n1, 2, 2 * h))
    w_data[:, 0, :h], w_data[:, 0, h:] = c1[:, :h], s1[:, :h]
    w_data[:, 1, :h], w_data[:, 1, h:] = -s1[:, :h], c1[:, :h]
    w_kern = np.stack([c1, -s1], axis=1)
    w_inv = np.zeros((2, h, n1, 2))
    w_inv[0, :, :, 0], w_inv[0, :, :, 1] = c1[:, :h].T, -s1[:, :h].T
    w_inv[1, :, :, 0], w_inv[1, :, :, 1] = s1[:, :h].T, c1[:, :h].T
    w_inv /= n
    t2 = np.arange(n2)[None, :]
    at = 2.0 * math.pi * k1 * t2 / n
    twr, twi = np.cos(at), -np.sin(at)
    k2 = np.arange(n2)[:, None]
    a2 = 2.0 * math.pi * k2 * t2 / n2
    c2, s2 = np.cos(a2), np.sin(a2)
    w2f = np.block([[c2, s2], [-s2, c2]])
    w2i = np.block([[c2, -s2], [s2, c2]])
    f = np.float32
    return dict(w_data=w_data.reshape(2 * n1, 2 * h).astype(f), w_kern=w_kern.reshape(2 * n1, n1).astype(f),
                w_inv=w_inv.reshape(2 * h, 2 * n1).astype(f), twr=twr[:, :, None].astype(f),
                twi=twi[:, :, None].astype(f), w2f=w2f.astype(f), w2i=w2i.astype(f))


DFT_SPLITS = 1
DFT_STORE = BF16 if DFT_SPLITS == 1 else F32
MID_KB = 4


def _split(x):
    x = x.astype(F32)
    hi = x.astype(BF16)
    if DFT_SPLITS == 1:
        return [hi]
    return [hi, (x - hi.astype(F32)).astype(BF16)]


def _dft_dot(a, b):
    acc = None
    for i, ai in enumerate(_split(a)):
        for j, bj in enumerate(_split(b)):
            if i + j < DFT_SPLITS:
                t = jnp.dot(ai, bj, preferred_element_type=F32)
                acc = t if acc is None else acc + t
    return acc


DFT_TB = 16
DFT_PITCH = 24
VLANES = 128


def _lmm_kernel(w_ref, z_ref, o_ref, zin_ref, out_ref):
    k, tb, ch = z_ref.shape[1:]
    m = o_ref.shape[1]
    nl = ch // VLANES
    pitch = DFT_PITCH
    for c in range(nl):
        lanes = slice(c * VLANES, (c + 1) * VLANES)
        for r in range(k):
            zin_ref[c, r * pitch:r * pitch + tb, :] = z_ref[0, r, :, lanes].astype(F32)
    w = w_ref[...]
    for t in range(tb):
        zt = jnp.concatenate([zin_ref[c, pl.ds(t, k, stride=pitch), :] for c in range(nl)], axis=-1)
        res = _dft_dot(w, zt)
        for c in range(nl):
            out_ref[c, pl.ds(t, m, stride=pitch), :] = res[:, c * VLANES:(c + 1) * VLANES]
    for c in range(nl):
        lanes = slice(c * VLANES, (c + 1) * VLANES)
        for r in range(m):
            o_ref[0, r, :, lanes] = out_ref[c, r * pitch:r * pitch + tb, :].astype(o_ref.dtype)


def _left_matmul(w, z, out_dtype):
    g, k, nt, ch = z.shape
    m = w.shape[0]
    tb = DFT_TB
    return pl.pallas_call(
        _lmm_kernel,
        grid=(g, nt // tb),
        in_specs=[pl.BlockSpec((m, k), lambda gi, j: (0, 0)),
                  pl.BlockSpec((1, k, tb, ch), lambda gi, j: (gi, 0, j, 0))],
        out_specs=pl.BlockSpec((1, m, tb, ch), lambda gi, j: (gi, 0, j, 0)),
        out_shape=jax.ShapeDtypeStruct((g, m, nt, ch), out_dtype),
        scratch_shapes=[pltpu.VMEM((ch // VLANES, k * DFT_PITCH, VLANES), F32),
                        pltpu.VMEM((ch // VLANES, m * DFT_PITCH, VLANES), F32)],
        compiler_params=_cparams(("parallel", "parallel")),
        name="dft_stage1",
    )(w, z)


def _twiddle_fwd(a, twr, twi):
    n2 = FFT_N2
    ar, ai = a[:n2].astype(F32), a[n2:].astype(F32)
    return ar * twr - ai * twi, ar * twi + ai * twr


def _stage2(w_ref, re, im):
    n2 = FFT_N2
    out = _dft_dot(w_ref[:, :n2], re) + _dft_dot(w_ref[:, n2:], im)
    return out[:n2], out[n2:]


def _kspec_kernel(a_ref, twr_ref, twi_ref, w2f_ref, o_ref):
    for t in range(MID_KB):
        br, bi = _twiddle_fwd(a_ref[0, t], twr_ref[t], twi_ref[t])
        xr, xi = _stage2(w2f_ref, br, bi)
        o_ref[t, :FFT_N2, :] = xr
        o_ref[t, FFT_N2:, :] = xi


def _fft_mid_kernel(a_ref, kf_ref, twr_ref, twi_ref, w2f_ref, w2i_ref, o_ref):
    n2 = FFT_N2
    for t in range(MID_KB):
        twr, twi = twr_ref[t], twi_ref[t]
        br, bi = _twiddle_fwd(a_ref[0, t], twr, twi)
        xr, xi = _stage2(w2f_ref, br, bi)
        kr, ki = kf_ref[t, :n2, :], kf_ref[t, n2:, :]
        yr = xr * kr - xi * ki
        yi = xr * ki + xi * kr
        cr, ci = _stage2(w2i_ref, yr, yi)
        o_ref[0, t, :n2, :] = (cr * twr + ci * twi).astype(o_ref.dtype)
        o_ref[0, t, n2:, :] = (ci * twr - cr * twi).astype(o_ref.dtype)


def _long_conv_latent(u, kern):
    b, l, ch = u.shape
    n1, n2 = FFT_N1, FFT_N2
    cst = _fft_constants()
    cols = n2 * ch
    kb = MID_KB
    tw_spec = pl.BlockSpec((kb, n2, 1), lambda k1, p: (k1, 0, 0))
    w2_spec = pl.BlockSpec((2 * n2, 2 * n2), lambda k1, p: (0, 0))
    ak = _left_matmul(jnp.asarray(cst["w_kern"]), kern.reshape(1, n1, n2, ch), DFT_STORE)
    kf = pl.pallas_call(
        _kspec_kernel,
        grid=(n1 // kb, 1),
        in_specs=[pl.BlockSpec((1, kb, 2 * n2, ch), lambda k1, p: (0, k1, 0, 0)), tw_spec, tw_spec, w2_spec],
        out_specs=pl.BlockSpec((kb, 2 * n2, ch), lambda k1, p: (k1, 0, 0)),
        out_shape=jax.ShapeDtypeStruct((n1, 2 * n2, ch), F32),
        compiler_params=_cparams(("parallel", "arbitrary")),
        name="dft_kernel_spectrum",
    )(ak.reshape(1, n1, 2 * n2, ch), jnp.asarray(cst["twr"]), jnp.asarray(cst["twi"]), jnp.asarray(cst["w2f"]))
    npair = b // 2
    a = _left_matmul(jnp.asarray(cst["w_data"]), u.reshape(npair, n1, n2, ch), DFT_STORE)
    d = pl.pallas_call(
        _fft_mid_kernel,
        grid=(n1 // kb, npair),
        in_specs=[pl.BlockSpec((1, kb, 2 * n2, ch), lambda k1, p: (p, k1, 0, 0)),
                  pl.BlockSpec((kb, 2 * n2, ch), lambda k1, p: (k1, 0, 0)),
                  tw_spec, tw_spec, w2_spec, w2_spec],
        out_specs=pl.BlockSpec((1, kb, 2 * n2, ch), lambda k1, p: (p, k1, 0, 0)),
        out_shape=jax.ShapeDtypeStruct((npair, n1, 2 * n2, ch), DFT_STORE),
        compiler_params=_cparams(("parallel", "arbitrary")),
        name="dft_mid",
    )(a.reshape(npair, n1, 2 * n2, ch), kf, jnp.asarray(cst["twr"]), jnp.asarray(cst["twi"]),
      jnp.asarray(cst["w2f"]), jnp.asarray(cst["w2i"]))
    y = _left_matmul(jnp.asarray(cst["w_inv"]), d.reshape(npair, 2 * n1, n2, ch), F32)
    return y.reshape(b, l, ch)


@functools.lru_cache(maxsize=None)
def _dft_constants(n):
    k = np.arange(n)[:, None]
    t = np.arange(n)[None, :]
    ang = 2.0 * math.pi * k * t / n
    return np.cos(ang).astype(np.float32), (-np.sin(ang)).astype(np.float32)


def _conv_ctx_kernel(u_ref, kern_ref, fr_ref, fi_ref, o_ref):
    l = u_ref.shape[1]
    n = 2 * l
    dot = functools.partial(jnp.dot, preferred_element_type=F32, precision=HIGHEST)
    fr, fi = fr_ref[...], fi_ref[...]
    kern = kern_ref[...]
    kr, ki = dot(fr, kern), dot(fi, kern)
    u = u_ref[0]
    ur, ui = dot(fr[:, :l], u), dot(fi[:, :l], u)
    yr = ur * kr - ui * ki
    yi = ur * ki + ui * kr
    o_ref[0] = (dot(fr[:l, :], yr) + dot(fi[:l, :], yi)) * (1.0 / n)


def _long_conv_ctx(u, kern):
    b, l, ch = u.shape
    n = 2 * l
    fr, fi = _dft_constants(n)
    return pl.pallas_call(
        _conv_ctx_kernel,
        grid=(b,),
        in_specs=[pl.BlockSpec((1, l, ch), lambda bi: (bi, 0, 0)),
                  pl.BlockSpec((n, ch), lambda bi: (0, 0)),
                  pl.BlockSpec((n, n), lambda bi: (0, 0)),
                  pl.BlockSpec((n, n), lambda bi: (0, 0))],
        out_specs=pl.BlockSpec((1, l, ch), lambda bi: (bi, 0, 0)),
        out_shape=jax.ShapeDtypeStruct((b, l, ch), F32),
        compiler_params=_cparams(("parallel",)),
        name="long_conv_ctx",
    )(u, kern, jnp.asarray(fr), jnp.asarray(fi))


def _mix_kernel(*refs, tm):
    it = iter(refs)
    x_ref, g1_ref = next(it), next(it)
    pa = [next(it) for _ in range(3)]
    pa_h = [next(it) for _ in range(6)]
    attn_ref, x0_ref, u_ref, yh_ref = next(it), next(it), next(it), next(it)
    pcf = [next(it) for _ in range(2)]
    pcf_h = [next(it) for _ in range(4)]
    pg = [next(it) for _ in range(8)]
    (conva_ref, wa_ref, wna_ref, skip_ref, wh_ref, convd_ref, convdb_ref, lng_ref, lnb_ref, wd_ref,
     wo_ref, o_ref) = [next(it) for _ in range(12)]

    i = pl.program_id(1)
    first = i == 0
    last = i == pl.num_programs(1) - 1
    dot = functools.partial(jnp.dot, preferred_element_type=F32)

    ext_x = _extend(pa[0][0], pa_h[0][0], pa_h[1][0], first, last)
    ext_c = _extend(pa[2][0], pa_h[4][0], pa_h[5][0], first, last)
    za = pa[1][0].astype(F32) * _dwconv_ext(ext_c * ext_x, conva_ref[...], tm)
    y_a = dot(za.astype(BF16), wa_ref[...])
    y_na = dot(attn_ref[0], wna_ref[...])
    u = u_ref[0]
    zh = x0_ref[0].astype(F32) * (yh_ref[0] + u * skip_ref[...])
    y_hy = dot(zh.astype(BF16), wh_ref[...])
    ext_a = _extend(pcf[0][0], pcf_h[0][0], pcf_h[1][0], first, last)
    ext_g = _extend(pcf[1][0], pcf_h[2][0], pcf_h[3][0], first, last)
    uc = _dwconv_ext(ext_a * ext_g, convd_ref[...], tm) + convdb_ref[...]
    mu = jnp.mean(uc, axis=-1, keepdims=True)
    dv = uc - mu
    var = jnp.mean(dv * dv, axis=-1, keepdims=True)
    zc = _silu(dv * lax.rsqrt(var + EPS) * lng_ref[...] + lnb_ref[...])
    y_cf = dot(zc.astype(BF16), wd_ref[...])

    halves = []
    for hf in range(2):
        sl = slice(hf * CB, (hf + 1) * CB)
        halves.append(pg[0 + hf][0] * y_a[:, sl].astype(BF16)
                      + pg[2 + hf][0] * y_na[:, sl].astype(BF16)
                      + pg[4 + hf][0] * y_hy[:, sl].astype(BF16)
                      + pg[6 + hf][0] * y_cf[:, sl].astype(BF16))
    merged = jnp.concatenate(halves, axis=-1)
    o_ref[0] = x_ref[0] + g1_ref[0] * dot(merged, wo_ref[...])


def _mix(x, g1, px, attn, x0, u, yh, wts, tm):
    b, s, d = x.shape
    row = lambda width: pl.BlockSpec((1, tm, width), lambda bi, i: (bi, i, 0))
    full2 = lambda a: pl.BlockSpec(a.shape, lambda bi, i: (0, 0))
    in_specs = ([row(d), pl.BlockSpec((1, 1, d), lambda bi, i: (bi, 0, 0))]
                + _col_specs(OFF_A, 3 * D_A, tm, s) + _halo_specs(OFF_A, 3 * D_A, tm, s)
                + [row(D_NA), row(D_HY), row(D_HY), row(D_HY)]
                + _col_specs(OFF_CF, 2 * D_CF, tm, s) + _halo_specs(OFF_CF, 2 * D_CF, tm, s)
                + _col_specs(OFF_G, N_BRANCH * D_MODEL, tm, s)
                + [full2(w) for w in wts])
    n_px = 3 + 6 + 2 + 4 + 8
    args = [x, g1] + [px] * 9 + [attn, x0, u, yh] + [px] * 14 + list(wts)
    assert len(args) == len(in_specs) and n_px == 23
    return pl.pallas_call(
        functools.partial(_mix_kernel, tm=tm),
        grid=(b, s // tm),
        in_specs=in_specs,
        out_specs=row(d),
        out_shape=jax.ShapeDtypeStruct((b, s, d), F32),
        compiler_params=_cparams(("parallel", "parallel")),
        name="mixer_out",
    )(*args)


FFN_CHUNK = 256


def _ffn_kernel(x_ref, g_ref, sh_ref, sc_ref, gate_ref, wa_ref, wu_ref, wo_ref, o_ref):
    x = x_ref[0]
    h = _mod_rmsnorm(x, g_ref[...], sh_ref[0], sc_ref[0]).astype(BF16)
    acc = None
    for c in range(wo_ref.shape[0] // FFN_CHUNK):
        cols = slice(c * FFN_CHUNK, (c + 1) * FFN_CHUNK)
        a = jnp.dot(h, wa_ref[:, cols], preferred_element_type=F32)
        up = jnp.dot(h, wu_ref[:, cols], preferred_element_type=F32)
        act = (_silu(a) * up).astype(BF16)
        part = jnp.dot(act, wo_ref[cols, :], preferred_element_type=F32)
        acc = part if acc is None else acc + part
    o_ref[0] = x + gate_ref[0] * acc


def _ffn(x, g, sh, sc, gate, w_in, w_out, tm):
    b, s, d = x.shape
    dff = w_out.shape[0]
    vec = pl.BlockSpec((1, 1, d), lambda bi, i: (bi, 0, 0))
    once = pl.Buffered(1)
    return pl.pallas_call(
        _ffn_kernel,
        grid=(b, s // tm),
        in_specs=[pl.BlockSpec((1, tm, d), lambda bi, i: (bi, i, 0)),
                  pl.BlockSpec((1, d), lambda bi, i: (0, 0)),
                  vec, vec, vec,
                  pl.BlockSpec((d, dff), lambda bi, i: (0, 0), pipeline_mode=once),
                  pl.BlockSpec((d, dff), lambda bi, i: (0, 1), pipeline_mode=once),
                  pl.BlockSpec((dff, d), lambda bi, i: (0, 0), pipeline_mode=once)],
        out_specs=pl.BlockSpec((1, tm, d), lambda bi, i: (bi, i, 0)),
        out_shape=jax.ShapeDtypeStruct((b, s, d), F32),
        compiler_params=_cparams(("parallel", "parallel")),
        name="ffn",
    )(x, g.reshape(1, d), sh, sc, gate, w_in, w_in, w_out)


def _heads(t):
    b, s, _ = t.shape
    return t.reshape(b, s, NA_HEADS, NA_HD).transpose(0, 2, 1, 3)


def _unheads(t):
    b, h, s, hd = t.shape
    return t.transpose(0, 2, 1, 3).reshape(b, s, h * hd)


def kernel(x, c, ctx, c_ctx, w_mod, b_mod, g_norm1, g_norm2, w_in, conv_a, w_a_out, q_gain, k_gain, rpb, w_na_out, conv_h, filt_w1, filt_b1, filt_w2, filt_b2, filt_w3, filt_b3, filt_w4, filt_freq, hy_skip, w_h_out, conv_d, conv_d_b, ln_g, ln_b, w_d_out, w_o, w_ffn_in, w_ffn_out):
    depth = w_mod.shape[0]
    b, s, d = x.shape
    n_ctx = ctx.shape[1]
    cc = jnp.zeros((8, d), F32).at[:b].set(c).at[b].set(c_ctx)

    for l in range(depth):
        last = l == depth - 1
        mods = _modulation(cc, w_mod[l], b_mod[l])
        lat = [mods[:b, t * d:(t + 1) * d].reshape(b, 1, d) for t in range(6)]
        cxm = [jnp.broadcast_to(mods[b:b + 1, t * d:(t + 1) * d].reshape(1, 1, d), (b, 1, d)) for t in range(6)]
        sh1, sc1, g1, sh2, sc2, g2 = lat
        csh1, csc1, cg1, csh2, csc2, cg2 = cxm

        w_in_b = w_in[l].astype(BF16)
        mix_w = (conv_a[l], w_a_out[l].astype(BF16), w_na_out[l].astype(BF16), hy_skip[l].reshape(1, -1),
                 w_h_out[l].astype(BF16), conv_d[l], conv_d_b[l].reshape(1, -1), ln_g[l].reshape(1, -1),
                 ln_b[l].reshape(1, -1), w_d_out[l].astype(BF16), w_o[l].astype(BF16))
        w_ffn_in_b = w_ffn_in[l].astype(BF16)
        w_ffn_out_b = w_ffn_out[l].astype(BF16)
        filt_args = (filt_w1[l], filt_b1[l], filt_w2[l], filt_b2[l], filt_w3[l], filt_b3[l], filt_w4[l], filt_freq[l])

        px = _in_proj(x, g_norm1[l], sh1, sc1, w_in_b, 512, SIG_FROM)
        if last:
            pcx = _in_proj(ctx, g_norm1[l], csh1, csc1, w_in_b[:, OFF_K:OFF_HY], n_ctx, 2 * D_NA // CB)
            kc_cb, vc_cb = 0, D_NA // LANES
        else:
            pc = pcx = _in_proj(ctx, g_norm1[l], csh1, csc1, w_in_b, n_ctx, SIG_FROM)
            kc_cb, vc_cb = OFF_K // LANES, OFF_V // LANES

        table = _bias_table(rpb[l])
        attn_x = _attn_latent(px, pcx, kc_cb, vc_cb, q_gain[l], k_gain[l], table)

        x0, u = _hy_pre(px, conv_h[l], 512)
        kern = _hyena_conv_kernel(s, *filt_args)
        yh = _long_conv_latent(u, kern)
        x_new = _mix(x, g1, px, attn_x, x0, u, yh, mix_w, 512)

        if not last:
            attn_c = _unheads(_attn_ctx(_heads(pc[..., OFF_Q:OFF_K]), _heads(pc[..., OFF_K:OFF_V]),
                                        _heads(pc[..., OFF_V:OFF_HY]), q_gain[l], k_gain[l]))
            x0c, uc = _hy_pre(pc, conv_h[l], n_ctx)
            kern_c = _hyena_conv_kernel(n_ctx, *filt_args)
            yhc = _long_conv_ctx(uc, kern_c)
            ctx = _mix(ctx, cg1, pc, attn_c, x0c, uc, yhc, mix_w, n_ctx)
            ctx = _ffn(ctx, g_norm2[l], csh2, csc2, cg2, w_ffn_in_b, w_ffn_out_b, n_ctx)

        x = _ffn(x_new, g_norm2[l], sh2, sc2, g2, w_ffn_in_b, w_ffn_out_b, 512)
    return x
```
